```python
import math
import jax, jax.numpy as jnp
from jax import lax
import numpy as np

D_MODEL = 2048
BATCH = 1
SEQ = 8192
DEPTH = 4
DEC_BATCH = 8
DEC_SEQ = 32
PAST_LEN = 1024

CHUNK = 64
D_PLE = 256
N_HEADS = 8
HEAD_DIM = 64
QK_DIM = 2 * HEAD_DIM
V_DIM = 2 * HEAD_DIM
D_ATTN = N_HEADS * V_DIM
D_RNN = D_MODEL - D_ATTN
N_RNN_BLOCKS = 8
RNN_BLOCK = D_RNN // N_RNN_BLOCKS
CONV_W = 4
RG_C = 8.0
N_GROUPS = 4
EXPERTS_PER_GROUP = 8
N_EXPERTS = N_GROUPS * EXPERTS_PER_GROUP
TOP_K = 2
D_EXPERT = 512
MOE_BLOCK = 128
Q_BLOCK = 128
D_IN = 2 * N_HEADS * QK_DIM + N_HEADS * V_DIM + 2 * D_RNN
SPLITS = (N_HEADS * QK_DIM, 2 * N_HEADS * QK_DIM, 2 * N_HEADS * QK_DIM + D_ATTN,
          2 * N_HEADS * QK_DIM + D_ATTN + D_RNN)
EPS = 1e-6
NEG_INF = -1e30

kernel_name = 'hybrid_stream_diffattn_rglru_hiermoe'


def rmsnorm(x, g):
    xf = x.astype(jnp.float32)
    y = xf * lax.rsqrt(jnp.mean(xf * xf, axis=-1, keepdims=True) + EPS)
    return (y * g.astype(jnp.float32)).astype(x.dtype)


def lambda_init(layer):
    return 0.8 - 0.6 * math.exp(-0.3 * layer)


def diff_attend(q, k, v, lam, mask):
    s = jnp.einsum('bqhcd,bkhcd->bhcqk', q, k).astype(jnp.float32) * (HEAD_DIM ** -0.5)
    if mask is not None:
        s = jnp.where(mask, s, NEG_INF)
    p = jax.nn.softmax(s, axis=-1)
    w = p[:, :, 0] - lam * p[:, :, 1]
    return jnp.einsum('bhqk,bkhv->bqhv', w.astype(v.dtype), v)


def attn_prompt(q, k, v, lam):
    B, S = q.shape[0], q.shape[1]
    nq = S // Q_BLOCK
    k_chunk = jnp.arange(S) // CHUNK
    qb = jnp.moveaxis(q.reshape(B, nq, Q_BLOCK, N_HEADS, 2, HEAD_DIM), 1, 0)

    def one_block(args):
        q_blk, i = args
        q_chunk = (i * Q_BLOCK + jnp.arange(Q_BLOCK)) // CHUNK
        mask = k_chunk[None, :] <= q_chunk[:, None]
        return diff_attend(q_blk, k, v, lam, mask)

    o = lax.map(one_block, (qb, jnp.arange(nq)))
    return jnp.moveaxis(o, 0, 1).reshape(B, S, N_HEADS, V_DIM)


def causal_conv(x, buf, w, b):
    L = x.shape[1]
    xp = jnp.concatenate([buf, x], axis=1)
    y = b + sum(xp[:, j:j + L] * w[j] for j in range(CONV_W))
    return y, xp[:, -(CONV_W - 1):]


def rg_lru(xc, h0, w_a, b_a, w_x, b_x, lam_param):
    B, L = xc.shape[0], xc.shape[1]
    xb = xc.reshape(B, L, N_RNN_BLOCKS, RNN_BLOCK)
    r = jax.nn.sigmoid(jnp.einsum('blnc,ncd->blnd', xb, w_a).reshape(B, L, D_RNN) + b_a)
    i = jax.nn.sigmoid(jnp.einsum('blnc,ncd->blnd', xb, w_x).reshape(B, L, D_RNN) + b_x)
    log_a = (-RG_C * r.astype(jnp.float32)) * jax.nn.softplus(-lam_param.astype(jnp.float32))
    a = jnp.exp(log_a)
    u = jnp.sqrt(-jnp.expm1(2.0 * log_a)) * (i * xc).astype(jnp.float32)
    u = u.at[:, 0].add(a[:, 0] * h0.astype(jnp.float32))

    def combine(e1, e2):
        a1, b1 = e1
        a2, b2 = e2
        return a1 * a2, a2 * b1 + b2

    _, hs = lax.associative_scan(combine, (a, u), axis=1)
    return hs.astype(xc.dtype)


def hier_moe(h, w_rg, b_rg, w_re, b_re, w1, w3, w2):
    B, L, D = h.shape
    T = B * L
    xt = h.reshape(T, D)
    gp = jax.nn.softmax((xt @ w_rg).astype(jnp.float32) + b_rg.astype(jnp.float32), axis=-1)
    g_val, g_idx = lax.top_k(gp, 1)
    el = ((xt @ w_re).astype(jnp.float32) + b_re.astype(jnp.float32)).reshape(T, N_GROUPS, EXPERTS_PER_GROUP)
    el_g = jnp.einsum('tg,tge->te', jax.nn.one_hot(g_idx[:, 0], N_GROUPS, dtype=jnp.float32), el)
    e_val, e_idx = lax.top_k(el_g, TOP_K)
    wts = g_val * jax.nn.softmax(e_val, axis=-1)
    eid = (g_idx * EXPERTS_PER_GROUP + e_idx).reshape(-1).astype(jnp.int32)
    A = T * TOP_K
    n_blocks = -(-(A + N_EXPERTS * (MOE_BLOCK - 1)) // MOE_BLOCK)
    n_rows = n_blocks * MOE_BLOCK
    tok = jnp.repeat(jnp.arange(T, dtype=jnp.int32), TOP_K)
    wt = wts.reshape(A)
    order = jnp.argsort(eid)
    eid_s, tok_s, wt_s = eid[order], tok[order], wt[order]
    counts = jnp.bincount(eid, length=N_EXPERTS).astype(jnp.int32)
    padded = (counts + MOE_BLOCK - 1) // MOE_BLOCK * MOE_BLOCK
    pad_end = jnp.cumsum(padded)
    pad_start = pad_end - padded
    start = jnp.cumsum(counts) - counts
    dest = pad_start[eid_s] + jnp.arange(A, dtype=jnp.int32) - start[eid_s]
    row_tok = jnp.zeros((n_rows,), jnp.int32).at[dest].set(tok_s)
    row_wt = jnp.zeros((n_rows,), h.dtype).at[dest].set(wt_s.astype(h.dtype))
    blk_exp = jnp.minimum(
        jnp.searchsorted(pad_end, jnp.arange(n_blocks, dtype=jnp.int32) * MOE_BLOCK, side='right'),
        N_EXPERTS - 1)
    xs = xt[row_tok].reshape(n_blocks, MOE_BLOCK, D)

    def expert_block(args):
        xb, e = args
        return (jax.nn.silu(xb @ w1[e]) * (xb @ w3[e])) @ w2[e]

    yb = lax.map(expert_block, (xs, blk_exp)).reshape(n_rows, D)
    y = jnp.zeros_like(xt).at[row_tok].add(yb * row_wt[:, None])
    return y.reshape(B, L, D)


def layer(x, p, k_past, v_past, h0, conv_buf, lw, l):
    B, L = x.shape[0], x.shape[1]
    lam0 = lambda_init(l)
    hn = rmsnorm(x, lw['norm_mix'])
    z = hn @ lw['w_in']
    q, k, v, xr, gate = jnp.split(z, SPLITS, axis=-1)
    q = rmsnorm(q.reshape(B, L, N_HEADS, 2, HEAD_DIM), lw['q_norm'])
    k = rmsnorm(k.reshape(B, L, N_HEADS, 2, HEAD_DIM), lw['k_norm'])
    v = v.reshape(B, L, N_HEADS, V_DIM)
    f32 = jnp.float32
    lam = (jnp.exp(jnp.sum(lw['lq1'].astype(f32) * lw['lk1'].astype(f32)))
           - jnp.exp(jnp.sum(lw['lq2'].astype(f32) * lw['lk2'].astype(f32))) + lam0)
    if k_past is None:
        o = attn_prompt(q, k, v, lam)
    else:
        k_all = jnp.concatenate([k_past.reshape(B, -1, N_HEADS, 2, HEAD_DIM).astype(k.dtype), k], axis=1)
        v_all = jnp.concatenate([v_past.astype(v.dtype), v], axis=1)
        o = diff_attend(q, k_all, v_all, lam, None)
    o = (rmsnorm(o, lw['attn_out_norm']) * (1.0 - lam0)).reshape(B, L, D_ATTN)
    xc, conv_tail = causal_conv(xr, conv_buf.astype(xr.dtype), lw['conv_w'], lw['conv_b'])
    hs = rg_lru(xc, h0, lw['w_rg_a'], lw['b_rg_a'], lw['w_rg_x'], lw['b_rg_x'], lw['rg_lambda'])
    y_r = rmsnorm(hs * jax.nn.gelu(gate), lw['rnn_out_norm'])
    x = x + jnp.concatenate([o, y_r], axis=-1) @ lw['w_out']
    x = x + hier_moe(rmsnorm(x, lw['norm_ffn']), lw['w_rg'], lw['b_rg'], lw['w_re'], lw['b_re'],
                     lw['w1'], lw['w3'], lw['w2'])
    g = jax.nn.sigmoid(rmsnorm(x, lw['norm_ple']) @ lw['w_ple_gate'])
    x = x + g * (p @ lw['w_ple_proj'])
    return x, k.reshape(B, L, N_HEADS, QK_DIM), v, hs[:, -1], conv_tail


def setup_inputs(seed: int = 0) -> dict:
    key = jax.random.key(seed)
    ks = iter(jax.random.split(key, 48))

    def nrm(shape, scale=1.0):
        return jax.random.normal(next(ks), shape, jnp.float32) * scale

    def gain(shape):
        return 1.0 + nrm(shape, 0.02)

    u = jax.random.uniform(next(ks), (DEPTH, D_RNN), jnp.float32, 0.9, 0.999)
    s = u ** (1.0 / RG_C)
    rg_lam = jnp.log(s) - jnp.log1p(-s)
    return {
        'x_prompt': nrm((BATCH, SEQ, D_MODEL)),
        'x_sample': nrm((DEC_BATCH, DEC_SEQ, D_MODEL)),
        'cache_k': nrm((DEPTH, DEC_BATCH, PAST_LEN, N_HEADS, QK_DIM)),
        'cache_v': nrm((DEPTH, DEC_BATCH, PAST_LEN, N_HEADS, V_DIM)),
        'state_rnn_h': nrm((DEPTH, DEC_BATCH, D_RNN), 0.5),
        'state_conv': nrm((DEPTH, DEC_BATCH, CONV_W - 1, D_RNN)),
        'p_prompt': nrm((DEPTH, BATCH, SEQ, D_PLE)),
        'p_sample': nrm((DEPTH, DEC_BATCH, DEC_SEQ, D_PLE)),
        'norm_mix': gain((DEPTH, D_MODEL)),
        'w_in': nrm((DEPTH, D_MODEL, D_IN), D_MODEL ** -0.5),
        'q_norm': gain((DEPTH, 2, HEAD_DIM)),
        'k_norm': gain((DEPTH, 2, HEAD_DIM)),
        'lambda_q1': nrm((DEPTH, HEAD_DIM), 0.1),
        'lambda_k1': nrm((DEPTH, HEAD_DIM), 0.1),
        'lambda_q2': nrm((DEPTH, HEAD_DIM), 0.1),
        'lambda_k2': nrm((DEPTH, HEAD_DIM), 0.1),
        'attn_out_norm': gain((DEPTH, V_DIM)),
        'conv_w': nrm((DEPTH, CONV_W, D_RNN), CONV_W ** -0.5),
        'conv_b': nrm((DEPTH, D_RNN), 0.01),
        'w_rg_a': nrm((DEPTH, N_RNN_BLOCKS, RNN_BLOCK, RNN_BLOCK), RNN_BLOCK ** -0.5),
        'b_rg_a': nrm((DEPTH, D_RNN), 0.01),
        'w_rg_x': nrm((DEPTH, N_RNN_BLOCKS, RNN_BLOCK, RNN_BLOCK), RNN_BLOCK ** -0.5),
        'b_rg_x': nrm((DEPTH, D_RNN), 0.01),
        'rg_lambda': rg_lam,
        'rnn_out_norm': gain((DEPTH, D_RNN)),
        'w_out': nrm((DEPTH, D_MODEL, D_MODEL), D_MODEL ** -0.5),
        'norm_ffn': gain((DEPTH, D_MODEL)),
        'w_router_group': nrm((DEPTH, D_MODEL, N_GROUPS), D_MODEL ** -0.5),
        'b_router_group': nrm((DEPTH, N_GROUPS), 0.01),
        'w_router_expert': nrm((DEPTH, D_MODEL, N_EXPERTS), D_MODEL ** -0.5),
        'b_router_expert': nrm((DEPTH, N_EXPERTS), 0.01),
        'w_exp_gate': nrm((DEPTH, N_EXPERTS, D_MODEL, D_EXPERT), D_MODEL ** -0.5),
        'w_exp_up': nrm((DEPTH, N_EXPERTS, D_MODEL, D_EXPERT), D_MODEL ** -0.5),
        'w_exp_down': nrm((DEPTH, N_EXPERTS, D_EXPERT, D_MODEL), D_EXPERT ** -0.5),
        'norm_ple': gain((DEPTH, D_MODEL)),
        'w_ple_gate': nrm((DEPTH, D_MODEL, D_MODEL), D_MODEL ** -0.5),
        'w_ple_proj': nrm((DEPTH, D_PLE, D_MODEL), D_PLE ** -0.5),
    }


def reference(x_prompt, x_sample, cache_k, cache_v, state_rnn_h, state_conv, p_prompt, p_sample,
              norm_mix, w_in, q_norm, k_norm, lambda_q1, lambda_k1, lambda_q2, lambda_k2,
              attn_out_norm, conv_w, conv_b, w_rg_a, b_rg_a, w_rg_x, b_rg_x, rg_lambda,
              rnn_out_norm, w_out, norm_ffn, w_router_group, b_router_group, w_router_expert,
              b_router_expert, w_exp_gate, w_exp_up, w_exp_down, norm_ple, w_ple_gate, w_ple_proj):
    y_p, y_s = x_prompt, x_sample
    Bp = x_prompt.shape[0]
    kp_l, vp_l, hp_l, cp_l = [], [], [], []
    ks_l, vs_l, hs_l, cs_l = [], [], [], []
    for l in range(DEPTH):
        lw = {
            'norm_mix': norm_mix[l], 'w_in': w_in[l], 'q_norm': q_norm[l], 'k_norm': k_norm[l],
            'lq1': lambda_q1[l], 'lk1': lambda_k1[l], 'lq2': lambda_q2[l], 'lk2': lambda_k2[l],
            'attn_out_norm': attn_out_norm[l], 'conv_w': conv_w[l], 'conv_b': conv_b[l],
            'w_rg_a': w_rg_a[l], 'b_rg_a': b_rg_a[l], 'w_rg_x': w_rg_x[l], 'b_rg_x': b_rg_x[l],
            'rg_lambda': rg_lambda[l], 'rnn_out_norm': rnn_out_norm[l], 'w_out': w_out[l],
            'norm_ffn': norm_ffn[l], 'w_rg': w_router_group[l], 'b_rg': b_router_group[l],
            'w_re': w_router_expert[l], 'b_re': b_router_expert[l], 'w1': w_exp_gate[l],
            'w3': w_exp_up[l], 'w2': w_exp_down[l], 'norm_ple': norm_ple[l],
            'w_ple_gate': w_ple_gate[l], 'w_ple_proj': w_ple_proj[l],
        }
        h0_p = jnp.zeros((Bp, D_RNN), y_p.dtype)
        buf_p = jnp.zeros((Bp, CONV_W - 1, D_RNN), y_p.dtype)
        y_p, kp, vp, hp, cp = layer(y_p, p_prompt[l], None, None, h0_p, buf_p, lw, l)
        y_s, ks_, vs_, hs_, cs_ = layer(y_s, p_sample[l], cache_k[l], cache_v[l],
                                        state_rnn_h[l], state_conv[l], lw, l)
        kp_l.append(kp); vp_l.append(vp); hp_l.append(hp); cp_l.append(cp)
        ks_l.append(ks_); vs_l.append(vs_); hs_l.append(hs_); cs_l.append(cs_)
    k_prompt, v_prompt = jnp.stack(kp_l), jnp.stack(vp_l)
    h_prompt, conv_prompt = jnp.stack(hp_l), jnp.stack(cp_l)
    k_sample, v_sample = jnp.stack(ks_l), jnp.stack(vs_l)
    h_sample, conv_sample = jnp.stack(hs_l), jnp.stack(cs_l)
    return (y_p, y_s, k_prompt, v_prompt, h_prompt, conv_prompt, k_sample, v_sample, h_sample, conv_sample)
```

```python
import functools
import math

import jax
import jax.numpy as jnp
from jax import lax
from jax.experimental import pallas as pl
from jax.experimental.pallas import tpu as pltpu

F32 = jnp.float32
BF16 = jnp.bfloat16

CHUNK = 64
N_HEADS = 8
HEAD_DIM = 64
QK_DIM = 2 * HEAD_DIM
V_DIM = 2 * HEAD_DIM
D_ATTN = N_HEADS * V_DIM
N_RNN_BLOCKS = 8
CONV_W = 4
RG_C = 8.0
N_GROUPS = 4
EXPERTS_PER_GROUP = 8
N_EXPERTS = N_GROUPS * EXPERTS_PER_GROUP
TOP_K = 2
EPS = 1e-6
NEG_INF = -1e30

LANES = 128
SUBLANES = 8
VMEM_LIMIT_BYTES = 56 * 1024 * 1024
MOE_ROWS = 256
ROUTER_LANES = 128


def _lambda_init(layer):
    return 0.8 - 0.6 * math.exp(-0.3 * layer)


def _row_tile(total, target, multiple=16):
    best = None
    for t in range(multiple, min(total, target) + 1, multiple):
        if total % t == 0:
            best = t
    assert best is not None, (total, target)
    return best


def _params(sem):
    return pltpu.CompilerParams(dimension_semantics=sem, vmem_limit_bytes=VMEM_LIMIT_BYTES)


def _rms(x, gain):
    ms = jnp.mean(x * x, axis=-1, keepdims=True)
    return x * lax.rsqrt(ms + EPS) * gain


def _half_norm(a, gain, lo):
    sq = a * a
    s_lo = jnp.sum(jnp.where(lo, sq, 0.0), axis=-1, keepdims=True)
    s_hi = jnp.sum(jnp.where(lo, 0.0, sq), axis=-1, keepdims=True)
    ms = jnp.where(lo, s_lo, s_hi) * (1.0 / HEAD_DIM)
    return a * lax.rsqrt(ms + EPS) * gain


def _in_proj_kernel(x_ref, g_ref, w_ref, qg_ref, kg_ref,
                    q_ref, k_ref, v_ref, xr_ref, gate_ref, hn_ref, *, steps_per_section, tn):
    j = pl.program_id(1)

    @pl.when(j == 0)
    def _():
        hn_ref[...] = _rms(x_ref[...], g_ref[...]).astype(BF16)

    acc = jnp.dot(hn_ref[...], w_ref[...].astype(BF16), preferred_element_type=F32)
    sec = j // steps_per_section
    lo = lax.broadcasted_iota(jnp.int32, (1, LANES), 1) < HEAD_DIM

    @pl.when(sec == 0)
    def _():
        for h in range(tn // LANES):
            sl = slice(h * LANES, (h + 1) * LANES)
            q_ref[:, sl] = (_half_norm(acc[:, sl], qg_ref[...], lo) * (HEAD_DIM ** -0.5)).astype(BF16)

    @pl.when(sec == 1)
    def _():
        for h in range(tn // LANES):
            sl = slice(h * LANES, (h + 1) * LANES)
            k_ref[:, sl] = _half_norm(acc[:, sl], kg_ref[...], lo)

    @pl.when(sec == 2)
    def _():
        v_ref[...] = acc

    @pl.when(sec == 3)
    def _():
        xr_ref[...] = acc

    @pl.when(sec == 4)
    def _():
        gate_ref[...] = acc


def _in_proj(x, g, w_in, qg, kg):
    T, D = x.shape
    tm = _row_tile(T, 768)
    tn = 512
    sec_w = D_ATTN
    sps = sec_w // tn
    n_steps = w_in.shape[1] // tn

    def out_spec(sec):
        return pl.BlockSpec((tm, tn), lambda i, j: (i, jnp.clip(j - sec * sps, 0, sps - 1)))

    return pl.pallas_call(
        functools.partial(_in_proj_kernel, steps_per_section=sps, tn=tn),
        grid=(T // tm, n_steps),
        in_specs=[
            pl.BlockSpec((tm, D), lambda i, j: (i, 0)),
            pl.BlockSpec((1, D), lambda i, j: (0, 0)),
            pl.BlockSpec((D, tn), lambda i, j: (0, j)),
            pl.BlockSpec((1, LANES), lambda i, j: (0, 0)),
            pl.BlockSpec((1, LANES), lambda i, j: (0, 0)),
        ],
        out_specs=[out_spec(s) for s in range(5)],
        out_shape=[jax.ShapeDtypeStruct((T, sec_w), BF16)] + [jax.ShapeDtypeStruct((T, sec_w), F32)] * 4,
        scratch_shapes=[pltpu.VMEM((tm, D), BF16)],
        compiler_params=_params(("parallel", "arbitrary")),
        name="in_proj",
    )(x, g, w_in, qg, kg)


def _lambda_value(lq1_ref, lk1_ref, lq2_ref, lk2_ref, lam0):
    s1 = jnp.sum(lq1_ref[...] * lk1_ref[...], axis=-1, keepdims=True)
    s2 = jnp.sum(lq2_ref[...] * lk2_ref[...], axis=-1, keepdims=True)
    return jnp.exp(s1) - jnp.exp(s2) + lam0


def _split_q(q):
    lo = lax.broadcasted_iota(jnp.int32, q.shape, 1) < HEAD_DIM
    zero = jnp.zeros_like(q)
    return jnp.concatenate([jnp.where(lo, q, zero), jnp.where(lo, zero, q)], axis=0)


def _scores(q2, kb):
    return lax.dot_general(q2, kb, (((1,), (1,)), ((), ())), preferred_element_type=F32)


def _finish_heads(acc, l, lam, gout, rows, lam0):
    o = acc[:rows] / l[:rows] - lam * (acc[rows:] / l[rows:])
    return _rms(o, gout) * (1.0 - lam0)


def _attn_prompt_kernel(qi_ref, kj_ref, q_ref, k_ref, v_ref, lq1_ref, lk1_ref, lq2_ref, lk2_ref, gout_ref,
                        o_ref, q2_sc, m_sc, l_sc, acc_sc, *, bq, lam0):
    p = pl.program_id(1)
    i = qi_ref[p]
    j = kj_ref[p]

    @pl.when(j == 0)
    def _():
        q2_sc[...] = _split_q(q_ref[...])
        m_sc[...] = jnp.full(m_sc.shape, NEG_INF, F32)
        l_sc[...] = jnp.zeros(l_sc.shape, F32)
        acc_sc[...] = jnp.zeros(acc_sc.shape, F32)

    kb = k_ref[...].astype(BF16)
    vb = v_ref[...].astype(BF16)

    def update(s):
        m_old = m_sc[...]
        m_new = jnp.maximum(m_old, jnp.max(s, axis=-1, keepdims=True))
        pr = jnp.exp(s - m_new)
        alpha = jnp.exp(m_old - m_new)
        l_sc[...] = alpha * l_sc[...] + jnp.sum(pr, axis=-1, keepdims=True)
        acc_sc[...] = alpha * acc_sc[...] + jnp.dot(pr.astype(BF16), vb, preferred_element_type=F32)
        m_sc[...] = m_new

    @pl.when(j < i)
    def _():
        update(_scores(q2_sc[...], kb))

    @pl.when(j == i)
    def _():
        s = _scores(q2_sc[...], kb)
        q_chunk = (lax.broadcasted_iota(jnp.int32, s.shape, 0) % bq) // CHUNK
        k_chunk = lax.broadcasted_iota(jnp.int32, s.shape, 1) // CHUNK
        update(jnp.where(k_chunk <= q_chunk, s, NEG_INF))
        lam = _lambda_value(lq1_ref, lk1_ref, lq2_ref, lk2_ref, lam0)
        o_ref[...] = _finish_heads(acc_sc[...], l_sc[...], lam, gout_ref[...], bq, lam0).astype(o_ref.dtype)


def _attn_prompt(q, k, v, lq1, lk1, lq2, lk2, gout, o_init, *, n_prompt, lam0):
    bq = _row_tile(n_prompt, 512, CHUNK)
    nq = n_prompt // bq
    pairs = [(i, j) for i in range(nq) for j in range(i + 1)]
    qi = jnp.asarray([a for a, _ in pairs], jnp.int32)
    kj = jnp.asarray([b for _, b in pairs], jnp.int32)
    small = lambda n: pl.BlockSpec((1, n), lambda h, p, qi, kj: (0, 0))
    grid_spec = pltpu.PrefetchScalarGridSpec(
        num_scalar_prefetch=2,
        grid=(N_HEADS, len(pairs)),
        in_specs=[
            pl.BlockSpec((bq, LANES), lambda h, p, qi, kj: (qi[p], h)),
            pl.BlockSpec((bq, LANES), lambda h, p, qi, kj: (kj[p], h)),
            pl.BlockSpec((bq, LANES), lambda h, p, qi, kj: (kj[p], h)),
            small(HEAD_DIM), small(HEAD_DIM), small(HEAD_DIM), small(HEAD_DIM), small(V_DIM),
            pl.BlockSpec(memory_space=pl.ANY),
        ],
        out_specs=pl.BlockSpec((bq, LANES), lambda h, p, qi, kj: (qi[p], h)),
        scratch_shapes=[
            pltpu.VMEM((2 * bq, LANES), BF16),
            pltpu.VMEM((2 * bq, 1), F32),
            pltpu.VMEM((2 * bq, 1), F32),
            pltpu.VMEM((2 * bq, LANES), F32),
        ],
    )

    def body(qi_ref, kj_ref, q_ref, k_ref, v_ref, a, b, c, d, g, o_init_ref, o_ref, *scratch):
        del o_init_ref
        _attn_prompt_kernel(qi_ref, kj_ref, q_ref, k_ref, v_ref, a, b, c, d, g, o_ref, *scratch, bq=bq, lam0=lam0)

    return pl.pallas_call(
        body,
        grid_spec=grid_spec,
        out_shape=jax.ShapeDtypeStruct(o_init.shape, o_init.dtype),
        input_output_aliases={10: 0},
        compiler_params=_params(("parallel", "arbitrary")),
        name="attn_prompt",
    )(qi, kj, q, k, v, lq1, lk1, lq2, lk2, gout, o_init)


def _attn_sample_kernel(q_ref, k_ref, v_ref, ck_ref, cv_ref, lq1_ref, lk1_ref, lq2_ref, lk2_ref, gout_ref,
                        o_init_ref, o_ref, *, rows, lam0):
    del o_init_ref
    lam = _lambda_value(lq1_ref, lk1_ref, lq2_ref, lk2_ref, lam0)
    for h in range(N_HEADS):
        sl = slice(h * LANES, (h + 1) * LANES)
        q2 = _split_q(q_ref[:, sl])
        s_c = _scores(q2, ck_ref[:, sl].astype(BF16))
        s_n = _scores(q2, k_ref[:, sl].astype(BF16))
        m = jnp.maximum(jnp.max(s_c, axis=-1, keepdims=True), jnp.max(s_n, axis=-1, keepdims=True))
        p_c = jnp.exp(s_c - m)
        p_n = jnp.exp(s_n - m)
        l = jnp.sum(p_c, axis=-1, keepdims=True) + jnp.sum(p_n, axis=-1, keepdims=True)
        acc = (jnp.dot(p_c.astype(BF16), cv_ref[:, sl].astype(BF16), preferred_element_type=F32)
               + jnp.dot(p_n.astype(BF16), v_ref[:, sl].astype(BF16), preferred_element_type=F32))
        o_ref[:, sl] = _finish_heads(acc, l, lam, gout_ref[...], rows, lam0).astype(o_ref.dtype)


def _attn_sample(q, k, v, cache_k, cache_v, layer, lq1, lk1, lq2, lk2, gout, o_init, *, n_prompt, n_streams, rows,
                 lam0):
    past = cache_k.shape[2]
    base = n_prompt // rows
    small = lambda n: pl.BlockSpec((1, n), lambda b: (0, 0))
    row_spec = pl.BlockSpec((rows, D_ATTN), lambda b: (base + b, 0))
    cache_spec = pl.BlockSpec((None, None, past, D_ATTN), lambda b: (layer, b, 0, 0))
    return pl.pallas_call(
        functools.partial(_attn_sample_kernel, rows=rows, lam0=lam0),
        grid=(n_streams,),
        in_specs=[row_spec, row_spec, row_spec, cache_spec, cache_spec,
                  small(HEAD_DIM), small(HEAD_DIM), small(HEAD_DIM), small(HEAD_DIM), small(V_DIM),
                  pl.BlockSpec(memory_space=pl.ANY)],
        out_specs=row_spec,
        out_shape=jax.ShapeDtypeStruct(o_init.shape, o_init.dtype),
        input_output_aliases={10: 0},
        compiler_params=_params(("parallel",)),
        name="attn_sample",
    )(q, k, v, cache_k, cache_v, lq1, lk1, lq2, lk2, gout, o_init)


def _gelu_tanh(x):
    return 0.5 * x * (1.0 + jnp.tanh(math.sqrt(2.0 / math.pi) * (x + 0.044715 * (x * x * x))))


def _rnn_kernel(xr_ref, gate_ref, h0_ref, cbuf_ref, cw_ref, cb_ref, wa_ref, ba_ref, wx_ref, bx_ref,
                lam_ref, gn_ref, y_init_ref, y_ref, hlast_ref, ctail_ref, h_sc, tail_sc, xc_sc, *, tb):
    del y_init_ref
    t = pl.program_id(1)

    @pl.when(t == 0)
    def _():
        h_sc[...] = h0_ref[...]
        tail_sc[...] = cbuf_ref[...]

    x = xr_ref[...]
    cw = cw_ref[...]
    cb = cb_ref[...]

    def conv(window, rows):
        acc = cb + window * cw[CONV_W - 1:CONV_W]
        for back in range(1, CONV_W):
            acc = acc + pltpu.roll(window, back, axis=0) * cw[CONV_W - 1 - back:CONV_W - back]
        return acc[-rows:]

    xc_sc[...] = conv(x, tb)
    head = jnp.concatenate([tail_sc[...], x[:SUBLANES]], axis=0)
    xc_sc[0:SUBLANES, :] = conv(head, SUBLANES)
    tail_sc[...] = x[tb - SUBLANES:]
    ctail_ref[...] = x[tb - SUBLANES:]
    xc = xc_sc[...]

    xb = xc.astype(BF16)
    r_parts, i_parts = [], []
    for n in range(N_RNN_BLOCKS):
        sl = slice(n * LANES, (n + 1) * LANES)
        r_parts.append(jnp.dot(xb[:, sl], wa_ref[n].astype(BF16), preferred_element_type=F32))
        i_parts.append(jnp.dot(xb[:, sl], wx_ref[n].astype(BF16), preferred_element_type=F32))
    r = jax.nn.sigmoid(jnp.concatenate(r_parts, axis=-1) + ba_ref[...])
    ig = jax.nn.sigmoid(jnp.concatenate(i_parts, axis=-1) + bx_ref[...])
    neg_lam = -lam_ref[...]
    softplus = jnp.maximum(neg_lam, 0.0) + jnp.log1p(jnp.exp(-jnp.abs(neg_lam)))
    log_a = (-RG_C * r) * softplus
    a = jnp.exp(log_a)
    u = jnp.sqrt(1.0 - a * a) * (ig * xc)

    row = lax.broadcasted_iota(jnp.int32, (tb, 1), 0)
    d = 1
    while d < tb:
        keep = row >= d
        u = jnp.where(keep, a * pltpu.roll(u, d, axis=0) + u, u)
        a = jnp.where(keep, a * pltpu.roll(a, d, axis=0), a)
        d *= 2
    hs = u + a * h_sc[...]
    h_sc[...] = hs[tb - 1:tb]
    hlast_ref[...] = hs[tb - 1:tb]

    y_ref[...] = _rms(hs * _gelu_tanh(gate_ref[...]), gn_ref[...]).astype(y_ref.dtype)


def _rnn(xr, gate, h0, cbuf, cw, cb, wa, ba, wx, bx, lam, gn, y_init, *, row0, n_seq, seq_len, tb):
    C = xr.shape[1]
    nb = seq_len // tb
    base = row0 // tb
    row_spec = pl.BlockSpec((tb, C), lambda b, t: (base + b * nb + t, 0))
    vec = lambda: pl.BlockSpec((1, C), lambda b, t: (0, 0))
    blk = lambda: pl.BlockSpec((N_RNN_BLOCKS, LANES, LANES), lambda b, t: (0, 0, 0))
    return pl.pallas_call(
        functools.partial(_rnn_kernel, tb=tb),
        grid=(n_seq, nb),
        in_specs=[row_spec, row_spec,
                  pl.BlockSpec((None, 1, C), lambda b, t: (b, 0, 0)),
                  pl.BlockSpec((None, SUBLANES, C), lambda b, t: (b, 0, 0)),
                  pl.BlockSpec((CONV_W, C), lambda b, t: (0, 0)), vec(),
                  blk(), vec(), blk(), vec(), vec(), vec(),
                  pl.BlockSpec(memory_space=pl.ANY)],
        out_specs=[row_spec,
                   pl.BlockSpec((None, 1, C), lambda b, t: (b, 0, 0)),
                   pl.BlockSpec((None, SUBLANES, C), lambda b, t: (b, 0, 0))],
        out_shape=[jax.ShapeDtypeStruct(y_init.shape, y_init.dtype),
                   jax.ShapeDtypeStruct((n_seq, 1, C), F32),
                   jax.ShapeDtypeStruct((n_seq, SUBLANES, C), F32)],
        scratch_shapes=[pltpu.VMEM((1, C), F32), pltpu.VMEM((SUBLANES, C), F32), pltpu.VMEM((tb, C), F32)],
        input_output_aliases={12: 0},
        compiler_params=_params(("arbitrary", "arbitrary")),
        name="rnn",
    )(xr, gate, h0, cbuf, cw, cb, wa, ba, wx, bx, lam, gn, y_init)


def _out_proj_kernel(o_ref, y_ref, wa_ref, wb_ref, x_ref, out_ref):
    acc = jnp.dot(o_ref[...], wa_ref[...].astype(BF16), preferred_element_type=F32)
    acc = acc + jnp.dot(y_ref[...], wb_ref[...].astype(BF16), preferred_element_type=F32)
    out_ref[...] = x_ref[...] + acc


def _out_proj(o, y, w_out, x):
    T, D = x.shape
    half = o.shape[1]
    tm = _row_tile(T, 768)
    tn = 512
    return pl.pallas_call(
        _out_proj_kernel,
        grid=(T // tm, D // tn),
        in_specs=[
            pl.BlockSpec((tm, half), lambda i, j: (i, 0)),
            pl.BlockSpec((tm, half), lambda i, j: (i, 0)),
            pl.BlockSpec((half, tn), lambda i, j: (0, j)),
            pl.BlockSpec((half, tn), lambda i, j: (1, j)),
            pl.BlockSpec((tm, tn), lambda i, j: (i, j)),
        ],
        out_specs=pl.BlockSpec((tm, tn), lambda i, j: (i, j)),
        out_shape=jax.ShapeDtypeStruct((T, D), F32),
        compiler_params=_params(("parallel", "arbitrary")),
        name="out_proj",
    )(o, y, w_out, w_out, x)


def _router_kernel(x_ref, g_ref, w_ref, b_ref, hn_ref, logit_ref):
    hn = _rms(x_ref[...], g_ref[...])
    hn_ref[...] = hn
    logit_ref[...] = jnp.dot(hn, w_ref[...], preferred_element_type=F32,
                             precision=lax.Precision.HIGHEST) + b_ref[...]


def _router(x, g, w, b):
    T, D = x.shape
    tm = _row_tile(T, 768)
    return pl.pallas_call(
        _router_kernel,
        grid=(T // tm,),
        in_specs=[pl.BlockSpec((tm, D), lambda i: (i, 0)),
                  pl.BlockSpec((1, D), lambda i: (0, 0)),
                  pl.BlockSpec((D, ROUTER_LANES), lambda i: (0, 0)),
                  pl.BlockSpec((1, ROUTER_LANES), lambda i: (0, 0))],
        out_specs=[pl.BlockSpec((tm, D), lambda i: (i, 0)),
                   pl.BlockSpec((tm, ROUTER_LANES), lambda i: (i, 0))],
        out_shape=[jax.ShapeDtypeStruct((T, D), F32), jax.ShapeDtypeStruct((T, ROUTER_LANES), F32)],
        compiler_params=_params(("parallel",)),
        name="router",
    )(x, g, w, b)


def _routing_tables(logits, b_group, b_expert, n_tokens):
    del b_group, b_expert
    T = n_tokens
    gl = logits[:, :N_GROUPS]
    gp = jax.nn.softmax(gl, axis=-1)
    g_val, g_idx = lax.top_k(gp, 1)
    el = logits[:, N_GROUPS:N_GROUPS + N_EXPERTS].reshape(T, N_GROUPS, EXPERTS_PER_GROUP)
    el_g = jnp.take_along_axis(el, g_idx[:, :, None], axis=1)[:, 0]
    e_val, e_idx = lax.top_k(el_g, TOP_K)
    wts = g_val * jax.nn.softmax(e_val, axis=-1)
    eid = (g_idx * EXPERTS_PER_GROUP + e_idx).astype(jnp.int32)

    A = T * TOP_K
    B = MOE_ROWS
    n_blocks = -(-(A + N_EXPERTS * (B - 1)) // B)
    n_rows = n_blocks * B
    eid_f = eid.reshape(A)
    onehot = (eid_f[:, None] == jnp.arange(N_EXPERTS, dtype=jnp.int32)[None, :]).astype(jnp.int32)
    rank = jnp.take_along_axis(jnp.cumsum(onehot, axis=0) - onehot, eid_f[:, None], axis=1)[:, 0]
    counts = jnp.sum(onehot, axis=0)
    padded = (counts + B - 1) // B * B
    pad_end = jnp.cumsum(padded)
    pad_start = pad_end - padded
    dest = pad_start[eid_f] + rank
    tok = jnp.arange(A, dtype=jnp.int32) // TOP_K
    slot = (jnp.arange(A, dtype=jnp.int32) % TOP_K) * T + tok
    row_src = jnp.zeros((n_rows,), jnp.int32).at[dest].set(tok)
    rows = jnp.arange(n_rows, dtype=jnp.int32)
    dump = 2 * T + ((rows // B) % 2) * B + rows % B
    row_dst = dump.at[dest].set(slot)
    row_wt = jnp.zeros((n_rows,), F32).at[dest].set(wts.reshape(A))
    blk_exp = jnp.minimum(
        jnp.searchsorted(pad_end, jnp.arange(n_blocks, dtype=jnp.int32) * B, side='right'),
        N_EXPERTS - 1).astype(jnp.int32)
    n_used = (pad_end[-1] // B).astype(jnp.int32).reshape(1)
    return blk_exp, row_src, row_dst, n_used, row_wt.reshape(n_rows, 1)


def _moe_kernel(blk_ref, src_ref, dst_ref, nused_ref, hn_ref, wt_ref, w1_ref, w3_ref, w2_ref,
                y_ref, xg_sc, yo_sc, gsem, ssem, *, n_blocks):
    del blk_ref
    B = MOE_ROWS
    n_real = y_ref.shape[0] - 2 * B
    i = pl.program_id(0)
    n_used = nused_ref[0]
    slot = i % 2

    def start_gather(blk, s):
        def body(r, c):
            src = src_ref[blk * B + r]
            pltpu.make_async_copy(hn_ref.at[pl.ds(src, 1)], xg_sc.at[s, pl.ds(r, 1)], gsem.at[s]).start()
            return c
        lax.fori_loop(0, B, body, 0, unroll=8)

    def wait_gather(s):
        pltpu.make_async_copy(hn_ref.at[pl.ds(0, B)], xg_sc.at[s], gsem.at[s]).wait()

    def start_scatter(blk, s):
        def body(r, c):
            dst = dst_ref[blk * B + r]
            pltpu.make_async_copy(yo_sc.at[s, pl.ds(r, 1)], y_ref.at[pl.ds(dst, 1)], ssem.at[s]).start()
            return c
        lax.fori_loop(0, B, body, 0, unroll=8)

    def wait_scatter(s):
        pltpu.make_async_copy(yo_sc.at[s], y_ref.at[pl.ds(0, B)], ssem.at[s]).wait()

    @pl.when(i == 0)
    def _():
        start_gather(0, 0)
        yo_sc[0] = jnp.zeros(yo_sc.shape[1:], F32)
        for half in range(2):
            fill = pltpu.make_async_copy(yo_sc.at[0], y_ref.at[pl.ds(n_real + half * B, B)], ssem.at[0])
            fill.start()
            fill.wait()

    @pl.when(i + 1 < n_used)
    def _():
        start_gather(i + 1, 1 - slot)

    @pl.when(i < n_used)
    def _():
        wait_gather(slot)

        @pl.when(i >= 2)
        def _():
            wait_scatter(slot)

        xb = xg_sc[slot].astype(BF16)
        h1 = jnp.dot(xb, w1_ref[...].astype(BF16), preferred_element_type=F32)
        h3 = jnp.dot(xb, w3_ref[...].astype(BF16), preferred_element_type=F32)
        h = (h1 * jax.nn.sigmoid(h1) * h3).astype(BF16)
        y = jnp.dot(h, w2_ref[...].astype(BF16), preferred_element_type=F32)
        yo_sc[slot] = y * wt_ref[...]
        start_scatter(i, slot)

    @pl.when(i == n_blocks - 1)
    def _():
        wait_scatter((n_used - 1) % 2)

        @pl.when(n_used >= 2)
        def _():
            wait_scatter(n_used % 2)


def _moe(hn, blk_exp, row_src, row_dst, n_used, row_wt, w1, w3, w2):
    T, D = hn.shape
    B = MOE_ROWS
    n_blocks = blk_exp.shape[0]
    de = w1.shape[2]
    grid_spec = pltpu.PrefetchScalarGridSpec(
        num_scalar_prefetch=4,
        grid=(n_blocks,),
        in_specs=[
            pl.BlockSpec(memory_space=pl.ANY),
            pl.BlockSpec((B, 1), lambda i, blk, src, dst, nu: (i, 0)),
            pl.BlockSpec((None, D, de), lambda i, blk, src, dst, nu: (blk[i], 0, 0)),
            pl.BlockSpec((None, D, de), lambda i, blk, src, dst, nu: (blk[i], 0, 0)),
            pl.BlockSpec((None, de, D), lambda i, blk, src, dst, nu: (blk[i], 0, 0)),
        ],
        out_specs=pl.BlockSpec(memory_space=pl.ANY),
        scratch_shapes=[
            pltpu.VMEM((2, B, D), F32),
            pltpu.VMEM((2, B, D), F32),
            pltpu.SemaphoreType.DMA((2,)),
            pltpu.SemaphoreType.DMA((2,)),
        ],
    )
    return pl.pallas_call(
        functools.partial(_moe_kernel, n_blocks=n_blocks),
        grid_spec=grid_spec,
        out_shape=jax.ShapeDtypeStruct((TOP_K * T + 2 * B, D), F32),
        compiler_params=_params(("arbitrary",)),
        name="moe",
    )(blk_exp, row_src, row_dst, n_used, hn, row_wt, w1, w3, w2)


def _ple_kernel(x_ref, y0_ref, y1_ref, g_ref, p_ref, wg_ref, wp_ref, out_ref, x2_sc, hn_sc, *, tn):
    j = pl.program_id(1)

    @pl.when(j == 0)
    def _():
        x2 = x_ref[...] + (y0_ref[...] + y1_ref[...])
        hn_sc[...] = _rms(x2, g_ref[...]).astype(BF16)
        for c in range(x2_sc.shape[0]):
            x2_sc[c] = x2[:, c * tn:(c + 1) * tn]

    gate = jax.nn.sigmoid(jnp.dot(hn_sc[...], wg_ref[...].astype(BF16), preferred_element_type=F32))
    proj = jnp.dot(p_ref[...].astype(BF16), wp_ref[...].astype(BF16), preferred_element_type=F32)
    out_ref[...] = x2_sc[j] + gate * proj


def _ple(x, y_slots, g, p, wg, wp):
    T, D = x.shape
    dp = p.shape[1]
    tm = _row_tile(T, 384)
    tn = 512
    nt = T // tm
    return pl.pallas_call(
        functools.partial(_ple_kernel, tn=tn),
        grid=(nt, D // tn),
        in_specs=[
            pl.BlockSpec((tm, D), lambda i, j: (i, 0)),
            pl.BlockSpec((tm, D), lambda i, j: (i, 0)),
            pl.BlockSpec((tm, D), lambda i, j: (i + nt, 0)),
            pl.BlockSpec((1, D), lambda i, j: (0, 0)),
            pl.BlockSpec((tm, dp), lambda i, j: (i, 0)),
            pl.BlockSpec((D, tn), lambda i, j: (0, j)),
            pl.BlockSpec((dp, tn), lambda i, j: (0, j)),
        ],
        out_specs=pl.BlockSpec((tm, tn), lambda i, j: (i, j)),
        out_shape=jax.ShapeDtypeStruct((T, D), F32),
        scratch_shapes=[pltpu.VMEM((D // tn, tm, tn), F32), pltpu.VMEM((tm, D), BF16)],
        compiler_params=_params(("parallel", "arbitrary")),
        name="ple",
    )(x, y_slots, y_slots, g, p, wg, wp)


def kernel(x_prompt, x_sample, cache_k, cache_v, state_rnn_h, state_conv, p_prompt, p_sample, norm_mix, w_in, q_norm, k_norm, lambda_q1, lambda_k1, lambda_q2, lambda_k2, attn_out_norm, conv_w, conv_b, w_rg_a, b_rg_a, w_rg_x, b_rg_x, rg_lambda, rnn_out_norm, w_out, norm_ffn, w_router_group, b_router_group, w_router_expert, b_router_expert, w_exp_gate, w_exp_up, w_exp_down, norm_ple, w_ple_gate, w_ple_proj):
    bp, seq, D = x_prompt.shape
    nb, dseq, _ = x_sample.shape
    depth = w_in.shape[0]
    past = cache_k.shape[2]
    d_rnn = state_rnn_h.shape[-1]
    d_ple = p_prompt.shape[-1]
    assert bp == 1 and seq % CHUNK == 0 and dseq % SUBLANES == 0 and seq % dseq == 0
    n_p = bp * seq
    n_s = nb * dseq
    T = n_p + n_s

    x = jnp.concatenate([x_prompt.reshape(n_p, D), x_sample.reshape(n_s, D)], axis=0)
    p_all = jnp.concatenate([p_prompt.reshape(depth, n_p, d_ple), p_sample.reshape(depth, n_s, d_ple)], axis=1)
    ck = cache_k.reshape(depth, nb, past, D_ATTN)
    cv = cache_v.reshape(depth, nb, past, D_ATTN)
    row = lambda a: a.reshape(1, -1)
    conv_pad = jnp.pad(state_conv, ((0, 0), (0, 0), (SUBLANES - (CONV_W - 1), 0), (0, 0)))
    zero_h = jnp.zeros((bp, 1, d_rnn), F32)
    zero_conv = jnp.zeros((bp, SUBLANES, d_rnn), F32)
    w_router = jnp.concatenate(
        [w_router_group, w_router_expert,
         jnp.zeros((depth, D, ROUTER_LANES - N_GROUPS - N_EXPERTS), F32)], axis=-1)
    b_router = jnp.concatenate(
        [b_router_group, b_router_expert, jnp.zeros((depth, ROUTER_LANES - N_GROUPS - N_EXPERTS), F32)], axis=-1)
    tb_p = _row_tile(seq, 256, SUBLANES)

    ks, vs, hp, cp, hs_, cs_ = [], [], [], [], [], []
    for l in range(depth):
        lam0 = _lambda_init(l)
        q, k, v, xr, gate = _in_proj(x, row(norm_mix[l]), w_in[l], row(q_norm[l]), row(k_norm[l]))
        lam_args = (row(lambda_q1[l]), row(lambda_k1[l]), row(lambda_q2[l]), row(lambda_k2[l]),
                    row(attn_out_norm[l]))
        o = jnp.zeros((T, D_ATTN), BF16)
        o = _attn_prompt(q, k, v, *lam_args, o, n_prompt=n_p, lam0=lam0)
        o = _attn_sample(q, k, v, ck, cv, l, *lam_args, o, n_prompt=n_p, n_streams=nb, rows=dseq, lam0=lam0)

        rnn_w = (conv_w[l], row(conv_b[l]), w_rg_a[l], row(b_rg_a[l]), w_rg_x[l], row(b_rg_x[l]),
                 row(rg_lambda[l]), row(rnn_out_norm[l]))
        y = jnp.zeros((T, d_rnn), BF16)
        y, h_p, c_p = _rnn(xr, gate, zero_h, zero_conv, *rnn_w, y, row0=0, n_seq=bp, seq_len=seq, tb=tb_p)
        y, h_s, c_s = _rnn(xr, gate, state_rnn_h[l].reshape(nb, 1, d_rnn), conv_pad[l], *rnn_w, y,
                           row0=n_p, n_seq=nb, seq_len=dseq, tb=dseq)

        x = _out_proj(o, y, w_out[l], x)

        hn, logits = _router(x, row(norm_ffn[l]), w_router[l], row(b_router[l]))
        blk_exp, row_src, row_dst, n_used, row_wt = _routing_tables(logits, None, None, T)
        y_slots = _moe(hn, blk_exp, row_src, row_dst, n_used, row_wt, w_exp_gate[l], w_exp_up[l], w_exp_down[l])

        x = _ple(x, y_slots, row(norm_ple[l]), p_all[l], w_ple_gate[l], w_ple_proj[l])

        ks.append(k)
        vs.append(v)
        hp.append(h_p[:, 0])
        cp.append(c_p[:, SUBLANES - (CONV_W - 1):])
        hs_.append(h_s[:, 0])
        cs_.append(c_s[:, SUBLANES - (CONV_W - 1):])

    k_all = jnp.stack(ks)
    v_all = jnp.stack(vs)
    return (x[:n_p].reshape(bp, seq, D), x[n_p:].reshape(nb, dseq, D),
            k_all[:, :n_p].reshape(depth, bp, seq, N_HEADS, QK_DIM),
            v_all[:, :n_p].reshape(depth, bp, seq, N_HEADS, V_DIM),
            jnp.stack(hp), jnp.stack(cp),
            k_all[:, n_p:].reshape(depth, nb, dseq, N_HEADS, QK_DIM),
            v_all[:, n_p:].reshape(depth, nb, dseq, N_HEADS, V_DIM),
            jnp.stack(hs_), jnp.stack(cs_))
```

```python
import functools
import math

import jax
import jax.numpy as jnp
from jax import lax
from jax.experimental import pallas as pl
from jax.experimental.pallas import tpu as pltpu

F32 = jnp.float32
BF16 = jnp.bfloat16

CHUNK = 64
N_HEADS = 8
HEAD_DIM = 64
QK_DIM = 2 * HEAD_DIM
V_DIM = 2 * HEAD_DIM
D_ATTN = N_HEADS * V_DIM
N_RNN_BLOCKS = 8
CONV_W = 4
RG_C = 8.0
N_GROUPS = 4
EXPERTS_PER_GROUP = 8
N_EXPERTS = N_GROUPS * EXPERTS_PER_GROUP
TOP_K = 2
EPS = 1e-6
NEG_INF = -1e30

LANES = 128
SUBLANES = 8
VMEM_LIMIT_BYTES = 56 * 1024 * 1024
MOE_ROWS = 256
ROUTER_LANES = 128


def _lambda_init(layer):
    return 0.8 - 0.6 * math.exp(-0.3 * layer)


def _row_tile(total, target, multiple=16):
    best = None
    for t in range(multiple, min(total, target) + 1, multiple):
        if total % t == 0:
            best = t
    assert best is not None, (total, target)
    return best


def _params(sem):
    return pltpu.CompilerParams(dimension_semantics=sem, vmem_limit_bytes=VMEM_LIMIT_BYTES)


def _layer_vec(l, n):
    return pl.BlockSpec((None, 1, n), lambda *_: (l, 0, 0))


def _rms(x, gain):
    ms = jnp.mean(x * x, axis=-1, keepdims=True)
    return x * lax.rsqrt(ms + EPS) * gain


def _half_norm(a, gain, lo):
    sq = a * a
    s_lo = jnp.sum(jnp.where(lo, sq, 0.0), axis=-1, keepdims=True)
    s_hi = jnp.sum(jnp.where(lo, 0.0, sq), axis=-1, keepdims=True)
    ms = jnp.where(lo, s_lo, s_hi) * (1.0 / HEAD_DIM)
    return a * lax.rsqrt(ms + EPS) * gain


def _in_proj_kernel(x_ref, g_ref, w_ref, qg_ref, kg_ref,
                    q_ref, k_ref, v_ref, xr_ref, gate_ref, kb_ref, vb_ref, hn_ref, *, steps_per_section, tn):
    j = pl.program_id(1)

    @pl.when(j == 0)
    def _():
        hn_ref[...] = _rms(x_ref[...], g_ref[...]).astype(BF16)

    acc = jnp.dot(hn_ref[...], w_ref[...].astype(BF16), preferred_element_type=F32)
    sec = j // steps_per_section
    lo = lax.broadcasted_iota(jnp.int32, (1, LANES), 1) < HEAD_DIM

    @pl.when(sec == 0)
    def _():
        for h in range(tn // LANES):
            sl = slice(h * LANES, (h + 1) * LANES)
            q_ref[:, sl] = (_half_norm(acc[:, sl], qg_ref[...], lo) * (HEAD_DIM ** -0.5)).astype(BF16)

    @pl.when(sec == 1)
    def _():
        for h in range(tn // LANES):
            sl = slice(h * LANES, (h + 1) * LANES)
            kn = _half_norm(acc[:, sl], kg_ref[...], lo)
            k_ref[:, sl] = kn
            kb_ref[:, sl] = kn.astype(BF16)

    @pl.when(sec == 2)
    def _():
        v_ref[...] = acc
        vb_ref[...] = acc.astype(BF16)

    @pl.when(sec == 3)
    def _():
        xr_ref[...] = acc

    @pl.when(sec == 4)
    def _():
        gate_ref[...] = acc


def _in_proj(x, g, w_in, qg, kg, *, layer):
    T, D = x.shape
    tm = _row_tile(T, 768)
    tn = 512
    sec_w = D_ATTN
    sps = sec_w // tn
    n_steps = w_in.shape[2] // tn
    col = lambda sec: (lambda j: jnp.clip(j - sec * sps, 0, sps - 1))

    def out_spec(sec):
        return pl.BlockSpec((tm, tn), lambda i, j: (i, col(sec)(j)))

    flat = jax.ShapeDtypeStruct((T, sec_w), F32)
    flat_b = jax.ShapeDtypeStruct((T, sec_w), BF16)
    return pl.pallas_call(
        functools.partial(_in_proj_kernel, steps_per_section=sps, tn=tn),
        grid=(T // tm, n_steps),
        in_specs=[
            pl.BlockSpec((tm, D), lambda i, j: (i, 0)),
            _layer_vec(layer, D),
            pl.BlockSpec((None, D, tn), lambda i, j: (layer, 0, j)),
            _layer_vec(layer, LANES),
            _layer_vec(layer, LANES),
        ],
        out_specs=[out_spec(s) for s in (0, 1, 2, 3, 4, 1, 2)],
        out_shape=[flat_b, flat, flat, flat, flat, flat_b, flat_b],
        scratch_shapes=[pltpu.VMEM((tm, D), BF16)],
        compiler_params=_params(("parallel", "arbitrary")),
        name="in_proj",
    )(x, g, w_in, qg, kg)


def _kv_layout_kernel(*refs, depth, n_prompt, n_sample):
    srcs = refs[:2 * depth]
    kp_ref, ks_ref, vp_ref, vs_ref, sem = refs[2 * depth:]
    copies = []
    for l in range(depth):
        for src, dst_p, dst_s in ((srcs[l], kp_ref, ks_ref), (srcs[depth + l], vp_ref, vs_ref)):
            for h in range(N_HEADS):
                cols = pl.ds(h * LANES, LANES)
                copies.append(pltpu.make_async_copy(
                    src.at[pl.ds(0, n_prompt), cols], dst_p.at[l, :, h, :], sem))
                copies.append(pltpu.make_async_copy(
                    src.at[pl.ds(n_prompt, n_sample), cols], dst_s.at[l, :, h, :], sem))
    for c in copies:
        c.start()
    for c in copies:
        c.wait()


def _kv_layout(ks, vs, *, n_prompt, n_sample):
    depth = len(ks)
    shape = lambda n: jax.ShapeDtypeStruct((depth, n, N_HEADS, LANES), F32)
    return pl.pallas_call(
        functools.partial(_kv_layout_kernel, depth=depth, n_prompt=n_prompt, n_sample=n_sample),
        in_specs=[pl.BlockSpec(memory_space=pl.ANY)] * (2 * depth),
        out_specs=[pl.BlockSpec(memory_space=pl.ANY)] * 4,
        out_shape=[shape(n_prompt), shape(n_sample), shape(n_prompt), shape(n_sample)],
        scratch_shapes=[pltpu.SemaphoreType.DMA(())],
        name="kv_layout",
    )(*ks, *vs)


def _lambda_value(lq1_ref, lk1_ref, lq2_ref, lk2_ref, lam0):
    s1 = jnp.sum(lq1_ref[...] * lk1_ref[...], axis=-1, keepdims=True)
    s2 = jnp.sum(lq2_ref[...] * lk2_ref[...], axis=-1, keepdims=True)
    return jnp.exp(s1) - jnp.exp(s2) + lam0


def _split_q(q):
    lo = lax.broadcasted_iota(jnp.int32, q.shape, 1) < HEAD_DIM
    zero = jnp.zeros_like(q)
    return jnp.concatenate([jnp.where(lo, q, zero), jnp.where(lo, zero, q)], axis=0)


def _scores(q2, kb):
    return lax.dot_general(q2, kb, (((1,), (1,)), ((), ())), preferred_element_type=F32)


def _finish_heads(acc, l, lam, gout, rows, lam0):
    o = acc[:rows] / l[:rows] - lam * (acc[rows:] / l[rows:])
    return _rms(o, gout) * (1.0 - lam0)


def _attn_prompt_kernel(q_ref, k_ref, v_ref, lq1_ref, lk1_ref, lq2_ref, lk2_ref, gout_ref, o_init_ref,
                        o_ref, q2_sc, m_sc, acc_sc, *, bq, lam0):
    del o_init_ref
    i = pl.program_id(1)
    bk = bq
    q2_sc[...] = _split_q(q_ref[...])
    m_sc[...] = jnp.full(m_sc.shape, NEG_INF, F32)
    acc_sc[...] = jnp.zeros(acc_sc.shape, F32)
    ones = jnp.ones((bk, LANES), BF16)

    def update(off, masked):
        kb = k_ref[pl.ds(off, bk), :]
        vb = v_ref[pl.ds(off, bk), :]
        s = _scores(q2_sc[...], kb)
        if masked:
            q_chunk = (lax.broadcasted_iota(jnp.int32, s.shape, 0) % bq) // CHUNK
            k_chunk = lax.broadcasted_iota(jnp.int32, s.shape, 1) // CHUNK
            s = jnp.where(k_chunk <= q_chunk, s, NEG_INF)
        m_old = m_sc[...]
        m_new = jnp.maximum(m_old, jnp.max(s, axis=-1, keepdims=True))
        pr = jnp.exp(s - jnp.concatenate([m_new] * (bk // LANES), axis=1))
        alpha = jnp.exp(m_old - m_new)
        pv = jnp.dot(pr.astype(BF16), jnp.concatenate([vb, ones], axis=1), preferred_element_type=F32)
        acc_sc[...] = jnp.concatenate([alpha, alpha], axis=1) * acc_sc[...] + pv
        m_sc[...] = m_new

    def body(j, carry):
        update(pl.multiple_of(j * bk, bk), False)
        return carry

    lax.fori_loop(0, i, body, 0)
    update(pl.multiple_of(i * bk, bk), True)
    lam = _lambda_value(lq1_ref, lk1_ref, lq2_ref, lk2_ref, lam0)
    acc = acc_sc[...]
    o_ref[...] = _finish_heads(acc[:, :LANES], acc[:, LANES:], lam, gout_ref[...], bq, lam0).astype(o_ref.dtype)


def _attn_prompt(q, kb, vb, lam_params, o_init, *, layer, n_prompt, lam0):
    bq = _row_tile(n_prompt, 1024, CHUNK)
    return pl.pallas_call(
        functools.partial(_attn_prompt_kernel, bq=bq, lam0=lam0),
        grid=(N_HEADS, n_prompt // bq),
        in_specs=[
            pl.BlockSpec((bq, LANES), lambda h, i: (i, h)),
            pl.BlockSpec((n_prompt, LANES), lambda h, i: (0, h)),
            pl.BlockSpec((n_prompt, LANES), lambda h, i: (0, h)),
            _layer_vec(layer, HEAD_DIM), _layer_vec(layer, HEAD_DIM), _layer_vec(layer, HEAD_DIM),
            _layer_vec(layer, HEAD_DIM), _layer_vec(layer, V_DIM),
            pl.BlockSpec(memory_space=pl.ANY),
        ],
        out_specs=pl.BlockSpec((bq, LANES), lambda h, i: (i, h)),
        out_shape=jax.ShapeDtypeStruct(o_init.shape, o_init.dtype),
        scratch_shapes=[
            pltpu.VMEM((2 * bq, LANES), BF16),
            pltpu.VMEM((2 * bq, LANES), F32),
            pltpu.VMEM((2 * bq, 2 * LANES), F32),
        ],
        input_output_aliases={8: 0},
        compiler_params=_params(("parallel", "arbitrary")),
        name="attn_prompt",
    )(q, kb, vb, *lam_params, o_init)


def _attn_sample_kernel(q_ref, k_ref, v_ref, ck_ref, cv_ref, lq1_ref, lk1_ref, lq2_ref, lk2_ref, gout_ref,
                        o_init_ref, o_ref, *, rows, lam0):
    del o_init_ref
    lam = _lambda_value(lq1_ref, lk1_ref, lq2_ref, lk2_ref, lam0)
    for h in range(N_HEADS):
        sl = slice(h * LANES, (h + 1) * LANES)
        q2 = _split_q(q_ref[:, sl])
        s_c = _scores(q2, ck_ref[:, h, :].astype(BF16))
        s_n = _scores(q2, k_ref[:, sl])
        m = jnp.maximum(jnp.max(s_c, axis=-1, keepdims=True), jnp.max(s_n, axis=-1, keepdims=True))
        p_c = jnp.exp(s_c - m)
        p_n = jnp.exp(s_n - m)
        l = jnp.sum(p_c, axis=-1, keepdims=True) + jnp.sum(p_n, axis=-1, keepdims=True)
        acc = (jnp.dot(p_c.astype(BF16), cv_ref[:, h, :].astype(BF16), preferred_element_type=F32)
               + jnp.dot(p_n.astype(BF16), v_ref[:, sl], preferred_element_type=F32))
        o_ref[:, sl] = _finish_heads(acc, l, lam, gout_ref[...], rows, lam0).astype(o_ref.dtype)


def _attn_sample(q, kb, vb, cache_k, cache_v, lam_params, o_init, *, layer, n_prompt, rows, lam0):
    n_streams, past = cache_k.shape[1], cache_k.shape[2]
    base = n_prompt // rows
    row_spec = pl.BlockSpec((rows, D_ATTN), lambda b: (base + b, 0))
    cache_spec = pl.BlockSpec((None, None, past, N_HEADS, LANES), lambda b: (layer, b, 0, 0, 0))
    return pl.pallas_call(
        functools.partial(_attn_sample_kernel, rows=rows, lam0=lam0),
        grid=(n_streams,),
        in_specs=[row_spec, row_spec, row_spec, cache_spec, cache_spec,
                  _layer_vec(layer, HEAD_DIM), _layer_vec(layer, HEAD_DIM), _layer_vec(layer, HEAD_DIM),
                  _layer_vec(layer, HEAD_DIM), _layer_vec(layer, V_DIM),
                  pl.BlockSpec(memory_space=pl.ANY)],
        out_specs=row_spec,
        out_shape=jax.ShapeDtypeStruct(o_init.shape, o_init.dtype),
        input_output_aliases={10: 0},
        compiler_params=_params(("parallel",)),
        name="attn_sample",
    )(q, kb, vb, cache_k, cache_v, *lam_params, o_init)


def _gelu_tanh(x):
    return 0.5 * x * (1.0 + jnp.tanh(math.sqrt(2.0 / math.pi) * (x + 0.044715 * (x * x * x))))


def _rnn_kernel(xr_ref, gate_ref, h0_ref, cbuf_ref, cw_ref, cb_ref, wa_ref, ba_ref, wx_ref, bx_ref,
                lam_ref, gn_ref, y_init_ref, y_ref, hlast_ref, ctail_ref, h_sc, tail_sc, xc_sc, *, tb):
    del y_init_ref
    t = pl.program_id(1)

    @pl.when(t == 0)
    def _():
        h_sc[...] = h0_ref[...]
        tail_sc[...] = cbuf_ref[...]

    x = xr_ref[...]
    cw = cw_ref[...]
    cb = cb_ref[...]

    def conv(window, rows):
        acc = cb + window * cw[CONV_W - 1:CONV_W]
        for back in range(1, CONV_W):
            acc = acc + pltpu.roll(window, back, axis=0) * cw[CONV_W - 1 - back:CONV_W - back]
        return acc[-rows:]

    xc_sc[...] = conv(x, tb)
    head = jnp.concatenate([tail_sc[...], x[:SUBLANES]], axis=0)
    xc_sc[0:SUBLANES, :] = conv(head, SUBLANES)
    tail_sc[...] = x[tb - SUBLANES:]
    ctail_ref[...] = x[tb - SUBLANES:]
    xc = xc_sc[...]

    xb = xc.astype(BF16)
    r_parts, i_parts = [], []
    for n in range(N_RNN_BLOCKS):
        sl = slice(n * LANES, (n + 1) * LANES)
        r_parts.append(jnp.dot(xb[:, sl], wa_ref[n].astype(BF16), preferred_element_type=F32))
        i_parts.append(jnp.dot(xb[:, sl], wx_ref[n].astype(BF16), preferred_element_type=F32))
    r = jax.nn.sigmoid(jnp.concatenate(r_parts, axis=-1) + ba_ref[...])
    ig = jax.nn.sigmoid(jnp.concatenate(i_parts, axis=-1) + bx_ref[...])
    neg_lam = -lam_ref[...]
    softplus = jnp.maximum(neg_lam, 0.0) + jnp.log1p(jnp.exp(-jnp.abs(neg_lam)))
    log_a = (-RG_C * r) * softplus
    a = jnp.exp(log_a)
    u = jnp.sqrt(1.0 - a * a) * (ig * xc)

    row = lax.broadcasted_iota(jnp.int32, (tb, 1), 0)
    d = 1
    while d < tb:
        keep = row >= d
        u = jnp.where(keep, a * pltpu.roll(u, d, axis=0) + u, u)
        a = jnp.where(keep, a * pltpu.roll(a, d, axis=0), a)
        d *= 2
    hs = u + a * h_sc[...]
    h_sc[...] = hs[tb - 1:tb]
    hlast_ref[...] = hs[tb - 1:tb]

    y_ref[...] = _rms(hs * _gelu_tanh(gate_ref[...]), gn_ref[...]).astype(y_ref.dtype)


def _rnn(xr, gate, h0, cbuf, weights, y_init, *, layer, state_layer, row0, n_seq, seq_len, tb):
    C = xr.shape[1]
    nb = seq_len // tb
    base = row0 // tb
    row_spec = pl.BlockSpec((tb, C), lambda b, t: (base + b * nb + t, 0))
    blk = lambda: pl.BlockSpec((None, N_RNN_BLOCKS, LANES, LANES), lambda b, t: (layer, 0, 0, 0))
    return pl.pallas_call(
        functools.partial(_rnn_kernel, tb=tb),
        grid=(n_seq, nb),
        in_specs=[row_spec, row_spec,
                  pl.BlockSpec((None, None, 1, C), lambda b, t: (state_layer, b, 0, 0)),
                  pl.BlockSpec((None, None, SUBLANES, C), lambda b, t: (state_layer, b, 0, 0)),
                  pl.BlockSpec((None, CONV_W, C), lambda b, t: (layer, 0, 0)), _layer_vec(layer, C),
                  blk(), _layer_vec(layer, C), blk(), _layer_vec(layer, C), _layer_vec(layer, C),
                  _layer_vec(layer, C),
                  pl.BlockSpec(memory_space=pl.ANY)],
        out_specs=[row_spec,
                   pl.BlockSpec((None, 1, C), lambda b, t: (b, 0, 0)),
                   pl.BlockSpec((None, SUBLANES, C), lambda b, t: (b, 0, 0))],
        out_shape=[jax.ShapeDtypeStruct(y_init.shape, y_init.dtype),
                   jax.ShapeDtypeStruct((n_seq, 1, C), F32),
                   jax.ShapeDtypeStruct((n_seq, SUBLANES, C), F32)],
        scratch_shapes=[pltpu.VMEM((1, C), F32), pltpu.VMEM((SUBLANES, C), F32), pltpu.VMEM((tb, C), F32)],
        input_output_aliases={12: 0},
        compiler_params=_params(("arbitrary", "arbitrary")),
        name="rnn",
    )(xr, gate, h0, cbuf, *weights, y_init)


def _out_proj_kernel(o_ref, y_ref, wa_ref, wb_ref, x_ref, out_ref):
    acc = jnp.dot(o_ref[...], wa_ref[...].astype(BF16), preferred_element_type=F32)
    acc = acc + jnp.dot(y_ref[...], wb_ref[...].astype(BF16), preferred_element_type=F32)
    out_ref[...] = x_ref[...] + acc


def _out_proj(o, y, w_out, x, *, layer):
    T, D = x.shape
    half = o.shape[1]
    tm = _row_tile(T, 768)
    tn = 512
    return pl.pallas_call(
        _out_proj_kernel,
        grid=(T // tm, D // tn),
        in_specs=[
            pl.BlockSpec((tm, half), lambda i, j: (i, 0)),
            pl.BlockSpec((tm, half), lambda i, j: (i, 0)),
            pl.BlockSpec((None, half, tn), lambda i, j: (layer, 0, j)),
            pl.BlockSpec((None, half, tn), lambda i, j: (layer, 1, j)),
            pl.BlockSpec((tm, tn), lambda i, j: (i, j)),
        ],
        out_specs=pl.BlockSpec((tm, tn), lambda i, j: (i, j)),
        out_shape=jax.ShapeDtypeStruct((T, D), F32),
        compiler_params=_params(("parallel", "arbitrary")),
        name="out_proj",
    )(o, y, w_out, w_out, x)


def _router_kernel(x_ref, g_ref, w_ref, b_ref, hn_ref, logit_ref):
    hn = _rms(x_ref[...], g_ref[...])
    hn_ref[...] = hn
    logit_ref[...] = jnp.dot(hn.astype(BF16), w_ref[...].astype(BF16), preferred_element_type=F32) + b_ref[...]


def _router(x, g, w, b, *, layer):
    T, D = x.shape
    tm = _row_tile(T, 768)
    return pl.pallas_call(
        _router_kernel,
        grid=(T // tm,),
        in_specs=[pl.BlockSpec((tm, D), lambda i: (i, 0)),
                  _layer_vec(layer, D),
                  pl.BlockSpec((None, D, ROUTER_LANES), lambda i: (layer, 0, 0)),
                  _layer_vec(layer, ROUTER_LANES)],
        out_specs=[pl.BlockSpec((tm, D), lambda i: (i, 0)),
                   pl.BlockSpec((tm, ROUTER_LANES), lambda i: (i, 0))],
        out_shape=[jax.ShapeDtypeStruct((T, D), F32), jax.ShapeDtypeStruct((T, ROUTER_LANES), F32)],
        compiler_params=_params(("parallel",)),
        name="router",
    )(x, g, w, b)


def _top1(x):
    val = jnp.max(x, axis=-1, keepdims=True)
    n = x.shape[-1]
    idx = jnp.min(jnp.where(x == val, lax.broadcasted_iota(jnp.int32, x.shape, x.ndim - 1), n), axis=-1,
                  keepdims=True)
    return val, idx


def _routing_tables(logits, n_tokens):
    T = n_tokens
    gp = jax.nn.softmax(logits[:, :N_GROUPS], axis=-1)
    g_val, g_idx = _top1(gp)
    el = logits[:, N_GROUPS:N_GROUPS + N_EXPERTS].reshape(T, N_GROUPS, EXPERTS_PER_GROUP)
    group = lax.broadcasted_iota(jnp.int32, el.shape, 1)
    el_g = jnp.sum(jnp.where(group == g_idx[:, :, None], el, 0.0), axis=1)
    v0, i0 = _top1(el_g)
    lane = lax.broadcasted_iota(jnp.int32, el_g.shape, 1)
    v1, i1 = _top1(jnp.where(lane == i0, -jnp.inf, el_g))
    e_val = jnp.concatenate([v0, v1], axis=-1)
    e_idx = jnp.concatenate([i0, i1], axis=-1)
    wts = g_val * jax.nn.softmax(e_val, axis=-1)
    eid = (g_idx * EXPERTS_PER_GROUP + e_idx).astype(jnp.int32)

    A = T * TOP_K
    B = MOE_ROWS
    n_blocks = -(-(A + N_EXPERTS * (B - 1)) // B)
    n_rows = n_blocks * B
    eid_f = eid.reshape(A)
    onehot = (eid_f[:, None] == jnp.arange(N_EXPERTS, dtype=jnp.int32)[None, :]).astype(jnp.int32)
    csum = jnp.cumsum(onehot, axis=0)
    rank = jnp.sum(onehot * csum, axis=1) - 1
    counts = csum[-1]
    padded = (counts + B - 1) // B * B
    pad_end = jnp.cumsum(padded)
    pad_start = pad_end - padded
    dest = jnp.sum(onehot * pad_start[None, :], axis=1) + rank
    row_a = jnp.full((n_rows,), -1, jnp.int32).at[dest].set(jnp.arange(A, dtype=jnp.int32))
    tok = jnp.maximum(row_a, 0) // TOP_K
    row_src = tok
    rows = jnp.arange(n_rows, dtype=jnp.int32)
    dump = TOP_K * T + ((rows // B) % 2) * B + rows % B
    row_dst = jnp.where(row_a < 0, dump, (row_a % TOP_K) * T + tok)
    blk_exp = jnp.minimum(
        jnp.sum((pad_end[None, :] <= (jnp.arange(n_blocks, dtype=jnp.int32) * B)[:, None]).astype(jnp.int32), axis=1),
        N_EXPERTS - 1).astype(jnp.int32)
    n_used = (pad_end[-1] // B).astype(jnp.int32).reshape(1)
    return blk_exp, row_src, row_dst, n_used, wts


def _moe_kernel(blk_ref, src_ref, dst_ref, nused_ref, hn_ref, w1_ref, w3_ref, w2_ref,
                y_ref, xg_sc, yo_sc, gsem, ssem, *, n_blocks):
    del blk_ref
    B = MOE_ROWS
    n_real = y_ref.shape[0] - 2 * B
    i = pl.program_id(0)
    n_used = nused_ref[0]
    slot = i % 2

    def start_gather(blk, s):
        def body(r, c):
            src = src_ref[blk * B + r]
            pltpu.make_async_copy(hn_ref.at[pl.ds(src, 1)], xg_sc.at[s, pl.ds(r, 1)], gsem.at[s]).start()
            return c
        lax.fori_loop(0, B, body, 0, unroll=8)

    def wait_gather(s):
        pltpu.make_async_copy(hn_ref.at[pl.ds(0, B)], xg_sc.at[s], gsem.at[s]).wait()

    def start_scatter(blk, s):
        def body(r, c):
            dst = dst_ref[blk * B + r]
            pltpu.make_async_copy(yo_sc.at[s, pl.ds(r, 1)], y_ref.at[pl.ds(dst, 1)], ssem.at[s]).start()
            return c
        lax.fori_loop(0, B, body, 0, unroll=8)

    def wait_scatter(s):
        pltpu.make_async_copy(yo_sc.at[s], y_ref.at[pl.ds(0, B)], ssem.at[s]).wait()

    @pl.when(i == 0)
    def _():
        start_gather(0, 0)
        yo_sc[0] = jnp.zeros(yo_sc.shape[1:], F32)
        for half in range(2):
            fill = pltpu.make_async_copy(yo_sc.at[0], y_ref.at[pl.ds(n_real + half * B, B)], ssem.at[0])
            fill.start()
            fill.wait()

    @pl.when(i + 1 < n_used)
    def _():
        start_gather(i + 1, 1 - slot)

    @pl.when(i < n_used)
    def _():
        wait_gather(slot)

        @pl.when(i >= 2)
        def _():
            wait_scatter(slot)

        xb = xg_sc[slot].astype(BF16)
        h1 = jnp.dot(xb, w1_ref[...].astype(BF16), preferred_element_type=F32)
        h3 = jnp.dot(xb, w3_ref[...].astype(BF16), preferred_element_type=F32)
        h = (h1 * jax.nn.sigmoid(h1) * h3).astype(BF16)
        yo_sc[slot] = jnp.dot(h, w2_ref[...].astype(BF16), preferred_element_type=F32)
        start_scatter(i, slot)

    @pl.when(i == n_blocks - 1)
    def _():
        wait_scatter((n_used - 1) % 2)

        @pl.when(n_used >= 2)
        def _():
            wait_scatter(n_used % 2)


def _moe(hn, blk_exp, row_src, row_dst, n_used, w1, w3, w2, *, layer):
    T, D = hn.shape
    B = MOE_ROWS
    n_blocks = blk_exp.shape[0]
    de = w1.shape[3]
    grid_spec = pltpu.PrefetchScalarGridSpec(
        num_scalar_prefetch=4,
        grid=(n_blocks,),
        in_specs=[
            pl.BlockSpec(memory_space=pl.ANY),
            pl.BlockSpec((None, None, D, de), lambda i, blk, src, dst, nu: (layer, blk[i], 0, 0)),
            pl.BlockSpec((None, None, D, de), lambda i, blk, src, dst, nu: (layer, blk[i], 0, 0)),
            pl.BlockSpec((None, None, de, D), lambda i, blk, src, dst, nu: (layer, blk[i], 0, 0)),
        ],
        out_specs=pl.BlockSpec(memory_space=pl.ANY),
        scratch_shapes=[
            pltpu.VMEM((2, B, D), F32),
            pltpu.VMEM((2, B, D), F32),
            pltpu.SemaphoreType.DMA((2,)),
            pltpu.SemaphoreType.DMA((2,)),
        ],
    )
    return pl.pallas_call(
        functools.partial(_moe_kernel, n_blocks=n_blocks),
        grid_spec=grid_spec,
        out_shape=jax.ShapeDtypeStruct((TOP_K * T + 2 * B, D), F32),
        compiler_params=_params(("arbitrary",)),
        name="moe",
    )(blk_exp, row_src, row_dst, n_used, hn, w1, w3, w2)


def _ple_kernel(x_ref, y0_ref, y1_ref, wt_ref, g_ref, p_ref, wg_ref, wp_ref, out_ref, x2_sc, hn_sc, *, tn):
    j = pl.program_id(1)

    @pl.when(j == 0)
    def _():
        wt = wt_ref[...]
        x2 = x_ref[...] + (wt[:, 0:1] * y0_ref[...] + wt[:, 1:2] * y1_ref[...])
        hn_sc[...] = _rms(x2, g_ref[...]).astype(BF16)
        for c in range(x2_sc.shape[0]):
            x2_sc[c] = x2[:, c * tn:(c + 1) * tn]

    gate = jax.nn.sigmoid(jnp.dot(hn_sc[...], wg_ref[...].astype(BF16), preferred_element_type=F32))
    proj = jnp.dot(p_ref[...].astype(BF16), wp_ref[...].astype(BF16), preferred_element_type=F32)
    out_ref[...] = x2_sc[j] + gate * proj


def _ple(x, y_slots, wts, g, p, wg, wp, *, layer):
    T, D = x.shape
    dp = p.shape[2]
    tm = _row_tile(T, 384)
    tn = 512
    nt = T // tm
    return pl.pallas_call(
        functools.partial(_ple_kernel, tn=tn),
        grid=(nt, D // tn),
        in_specs=[
            pl.BlockSpec((tm, D), lambda i, j: (i, 0)),
            pl.BlockSpec((tm, D), lambda i, j: (i, 0)),
            pl.BlockSpec((tm, D), lambda i, j: (i + nt, 0)),
            pl.BlockSpec((tm, TOP_K), lambda i, j: (i, 0)),
            _layer_vec(layer, D),
            pl.BlockSpec((None, tm, dp), lambda i, j: (layer, i, 0)),
            pl.BlockSpec((None, D, tn), lambda i, j: (layer, 0, j)),
            pl.BlockSpec((None, dp, tn), lambda i, j: (layer, 0, j)),
        ],
        out_specs=pl.BlockSpec((tm, tn), lambda i, j: (i, j)),
        out_shape=jax.ShapeDtypeStruct((T, D), F32),
        scratch_shapes=[pltpu.VMEM((D // tn, tm, tn), F32), pltpu.VMEM((tm, D), BF16)],
        compiler_params=_params(("parallel", "arbitrary")),
        name="ple",
    )(x, y_slots, y_slots, wts, g, p, wg, wp)


def kernel(x_prompt, x_sample, cache_k, cache_v, state_rnn_h, state_conv, p_prompt, p_sample, norm_mix, w_in, q_norm, k_norm, lambda_q1, lambda_k1, lambda_q2, lambda_k2, attn_out_norm, conv_w, conv_b, w_rg_a, b_rg_a, w_rg_x, b_rg_x, rg_lambda, rnn_out_norm, w_out, norm_ffn, w_router_group, b_router_group, w_router_expert, b_router_expert, w_exp_gate, w_exp_up, w_exp_down, norm_ple, w_ple_gate, w_ple_proj):
    bp, seq, D = x_prompt.shape
    nb, dseq, _ = x_sample.shape
    depth = w_in.shape[0]
    d_rnn = state_rnn_h.shape[-1]
    d_ple = p_prompt.shape[-1]
    assert bp == 1 and seq % CHUNK == 0 and dseq % 16 == 0 and seq % dseq == 0
    n_p = bp * seq
    n_s = nb * dseq
    T = n_p + n_s

    x = jnp.concatenate([x_prompt.reshape(n_p, D), x_sample.reshape(n_s, D)], axis=0)
    p_all = jnp.concatenate([p_prompt.reshape(depth, n_p, d_ple), p_sample.reshape(depth, n_s, d_ple)], axis=1)
    vec = lambda a: a.reshape(depth, 1, -1)
    conv_pad = jnp.pad(state_conv, ((0, 0), (0, 0), (SUBLANES - (CONV_W - 1), 0), (0, 0)))
    h0_s = state_rnn_h.reshape(depth, nb, 1, d_rnn)
    zero_h = jnp.zeros((1, bp, 1, d_rnn), F32)
    zero_conv = jnp.zeros((1, bp, SUBLANES, d_rnn), F32)
    pad_lanes = ROUTER_LANES - N_GROUPS - N_EXPERTS
    w_router = jnp.concatenate([w_router_group, w_router_expert, jnp.zeros((depth, D, pad_lanes), F32)], axis=-1)
    b_router = jnp.concatenate([b_router_group, b_router_expert, jnp.zeros((depth, pad_lanes), F32)], axis=-1)
    tb_p = _row_tile(seq, 256, SUBLANES)

    norm_mix_v, qg_v, kg_v = vec(norm_mix), vec(q_norm), vec(k_norm)
    lam_params = (vec(lambda_q1), vec(lambda_k1), vec(lambda_q2), vec(lambda_k2), vec(attn_out_norm))
    rnn_w = (conv_w, vec(conv_b), w_rg_a, vec(b_rg_a), w_rg_x, vec(b_rg_x), vec(rg_lambda), vec(rnn_out_norm))
    norm_ffn_v, b_router_v, norm_ple_v = vec(norm_ffn), vec(b_router), vec(norm_ple)

    ks, vs = [], []
    hp, cp, hs_, cs_ = [], [], [], []
    for l in range(depth):
        lam0 = _lambda_init(l)
        q, k, v, xr, gate, kb, vb = _in_proj(x, norm_mix_v, w_in, qg_v, kg_v, layer=l)
        ks.append(k)
        vs.append(v)
        o = jnp.zeros((T, D_ATTN), BF16)
        o = _attn_prompt(q, kb, vb, lam_params, o, layer=l, n_prompt=n_p, lam0=lam0)
        o = _attn_sample(q, kb, vb, cache_k, cache_v, lam_params, o, layer=l, n_prompt=n_p, rows=dseq, lam0=lam0)

        y = jnp.zeros((T, d_rnn), BF16)
        y, h_p, c_p = _rnn(xr, gate, zero_h, zero_conv, rnn_w, y, layer=l, state_layer=0,
                           row0=0, n_seq=bp, seq_len=seq, tb=tb_p)
        y, h_s, c_s = _rnn(xr, gate, h0_s, conv_pad, rnn_w, y, layer=l, state_layer=l,
                           row0=n_p, n_seq=nb, seq_len=dseq, tb=dseq)

        x = _out_proj(o, y, w_out, x, layer=l)

        hn, logits = _router(x, norm_ffn_v, w_router, b_router_v, layer=l)
        blk_exp, row_src, row_dst, n_used, wts = _routing_tables(logits, T)
        y_slots = _moe(hn, blk_exp, row_src, row_dst, n_used, w_exp_gate, w_exp_up, w_exp_down, layer=l)

        x = _ple(x, y_slots, wts, norm_ple_v, p_all, w_ple_gate, w_ple_proj, layer=l)

        hp.append(h_p[:, 0])
        cp.append(c_p[:, SUBLANES - (CONV_W - 1):])
        hs_.append(h_s[:, 0])
        cs_.append(c_s[:, SUBLANES - (CONV_W - 1):])

    k_p, k_s, v_p, v_s = _kv_layout(ks, vs, n_prompt=n_p, n_sample=n_s)
    return (x[:n_p].reshape(bp, seq, D), x[n_p:].reshape(nb, dseq, D),
            k_p.reshape(depth, bp, seq, N_HEADS, QK_DIM), v_p.reshape(depth, bp, seq, N_HEADS, V_DIM),
            jnp.stack(hp), jnp.stack(cp),
            k_s.reshape(depth, nb, dseq, N_HEADS, QK_DIM), v_s.reshape(depth, nb, dseq, N_HEADS, V_DIM),
            jnp.stack(hs_), jnp.stack(cs_))
```

```python
import functools
import math

import jax
import jax.numpy as jnp
from jax import lax
from jax.experimental import pallas as pl
from jax.experimental.pallas import tpu as pltpu

F32 = jnp.float32
BF16 = jnp.bfloat16

CHUNK = 64
N_HEADS = 8
HEAD_DIM = 64
QK_DIM = 2 * HEAD_DIM
V_DIM = 2 * HEAD_DIM
D_ATTN = N_HEADS * V_DIM
N_RNN_BLOCKS = 8
CONV_W = 4
RG_C = 8.0
N_GROUPS = 4
EXPERTS_PER_GROUP = 8
N_EXPERTS = N_GROUPS * EXPERTS_PER_GROUP
TOP_K = 2
EPS = 1e-6
NEG_INF = -1e30

LANES = 128
SUBLANES = 8
VMEM_LIMIT_BYTES = 56 * 1024 * 1024
MOE_ROWS = 256
ROUTER_LANES = 128


def _lambda_init(layer):
    return 0.8 - 0.6 * math.exp(-0.3 * layer)


def _row_tile(total, target, multiple=16):
    best = None
    for t in range(multiple, min(total, target) + 1, multiple):
        if total % t == 0:
            best = t
    assert best is not None, (total, target)
    return best


def _params(sem):
    return pltpu.CompilerParams(dimension_semantics=sem, vmem_limit_bytes=VMEM_LIMIT_BYTES)


def _layer_vec(l, n):
    return pl.BlockSpec((None, 1, n), lambda *_: (l, 0, 0))


def _rms(x, gain):
    ms = jnp.mean(x * x, axis=-1, keepdims=True)
    return x * lax.rsqrt(ms + EPS) * gain


def _half_norm(a, gain, lo):
    sq = a * a
    s_lo = jnp.sum(jnp.where(lo, sq, 0.0), axis=-1, keepdims=True)
    s_hi = jnp.sum(jnp.where(lo, 0.0, sq), axis=-1, keepdims=True)
    ms = jnp.where(lo, s_lo, s_hi) * (1.0 / HEAD_DIM)
    return a * lax.rsqrt(ms + EPS) * gain


def _in_proj_kernel(x_ref, g_ref, w_ref, qg_ref, kg_ref,
                    q_ref, k_ref, v_ref, xr_ref, gate_ref, kb_ref, vb_ref, hn_ref, *, steps_per_section, tn):
    j = pl.program_id(1)

    @pl.when(j == 0)
    def _():
        hn_ref[...] = _rms(x_ref[...], g_ref[...]).astype(BF16)

    acc = jnp.dot(hn_ref[...], w_ref[...].astype(BF16), preferred_element_type=F32)
    sec = j // steps_per_section
    lo = lax.broadcasted_iota(jnp.int32, (1, LANES), 1) < HEAD_DIM

    @pl.when(sec == 0)
    def _():
        for h in range(tn // LANES):
            sl = slice(h * LANES, (h + 1) * LANES)
            q_ref[:, sl] = (_half_norm(acc[:, sl], qg_ref[...], lo) * (HEAD_DIM ** -0.5)).astype(BF16)

    @pl.when(sec == 1)
    def _():
        for h in range(tn // LANES):
            sl = slice(h * LANES, (h + 1) * LANES)
            kn = _half_norm(acc[:, sl], kg_ref[...], lo)
            k_ref[:, sl] = kn
            kb_ref[:, sl] = kn.astype(BF16)

    @pl.when(sec == 2)
    def _():
        v_ref[...] = acc
        vb_ref[...] = acc.astype(BF16)

    @pl.when(sec == 3)
    def _():
        xr_ref[...] = acc

    @pl.when(sec == 4)
    def _():
        gate_ref[...] = acc


def _in_proj(x, g, w_in, qg, kg, *, layer):
    T, D = x.shape
    tm = _row_tile(T, 768)
    tn = 512
    sec_w = D_ATTN
    sps = sec_w // tn
    n_steps = w_in.shape[2] // tn
    col = lambda sec: (lambda j: jnp.clip(j - sec * sps, 0, sps - 1))

    def out_spec(sec):
        return pl.BlockSpec((tm, tn), lambda i, j: (i, col(sec)(j)))

    flat = jax.ShapeDtypeStruct((T, sec_w), F32)
    flat_b = jax.ShapeDtypeStruct((T, sec_w), BF16)
    return pl.pallas_call(
        functools.partial(_in_proj_kernel, steps_per_section=sps, tn=tn),
        grid=(T // tm, n_steps),
        in_specs=[
            pl.BlockSpec((tm, D), lambda i, j: (i, 0)),
            _layer_vec(layer, D),
            pl.BlockSpec((None, D, tn), lambda i, j: (layer, 0, j)),
            _layer_vec(layer, LANES),
            _layer_vec(layer, LANES),
        ],
        out_specs=[out_spec(s) for s in (0, 1, 2, 3, 4, 1, 2)],
        out_shape=[flat_b, flat, flat, flat, flat, flat_b, flat_b],
        scratch_shapes=[pltpu.VMEM((tm, D), BF16)],
        compiler_params=_params(("parallel", "arbitrary")),
        name="in_proj",
    )(x, g, w_in, qg, kg)


def _kv_layout_kernel(*refs, depth, prompt_tiles):
    srcs = refs[:2 * depth]
    kp_ref, ks_ref, vp_ref, vs_ref = refs[2 * depth:]
    l = pl.program_id(0)
    i = pl.program_id(1)

    def spread(src, dst):
        for h in range(N_HEADS):
            dst[:, h, :] = src[:, h * LANES:(h + 1) * LANES]

    for d in range(depth):
        @pl.when(jnp.logical_and(l == d, i < prompt_tiles))
        def _():
            spread(srcs[d], kp_ref)
            spread(srcs[depth + d], vp_ref)

        @pl.when(jnp.logical_and(l == d, i >= prompt_tiles))
        def _():
            spread(srcs[d], ks_ref)
            spread(srcs[depth + d], vs_ref)


def _kv_layout(ks, vs, *, n_prompt, n_sample):
    depth = len(ks)
    T, width = ks[0].shape
    tr = math.gcd(n_prompt, n_sample)
    nt, np_t, ns_t = T // tr, n_prompt // tr, n_sample // tr

    def src_spec(d):
        return pl.BlockSpec((tr, width), lambda l, i: (jnp.where(l < d, 0, jnp.where(l > d, nt - 1, i)), 0))

    blk = (None, tr, N_HEADS, LANES)
    prompt_spec = pl.BlockSpec(blk, lambda l, i: (l, jnp.minimum(i, np_t - 1), 0, 0))
    sample_spec = pl.BlockSpec(blk, lambda l, i: (l, jnp.clip(i - np_t, 0, ns_t - 1), 0, 0))
    shape = lambda n: jax.ShapeDtypeStruct((depth, n, N_HEADS, LANES), F32)
    return pl.pallas_call(
        functools.partial(_kv_layout_kernel, depth=depth, prompt_tiles=np_t),
        grid=(depth, nt),
        in_specs=[src_spec(d) for d in range(depth)] * 2,
        out_specs=[prompt_spec, sample_spec, prompt_spec, sample_spec],
        out_shape=[shape(n_prompt), shape(n_sample), shape(n_prompt), shape(n_sample)],
        compiler_params=_params(("arbitrary", "arbitrary")),
        name="kv_layout",
    )(*ks, *vs)


def _lambda_value(lq1_ref, lk1_ref, lq2_ref, lk2_ref, lam0):
    s1 = jnp.sum(lq1_ref[...] * lk1_ref[...], axis=-1, keepdims=True)
    s2 = jnp.sum(lq2_ref[...] * lk2_ref[...], axis=-1, keepdims=True)
    return jnp.exp(s1) - jnp.exp(s2) + lam0


def _split_q(q):
    lo = lax.broadcasted_iota(jnp.int32, q.shape, 1) < HEAD_DIM
    zero = jnp.zeros_like(q)
    return jnp.concatenate([jnp.where(lo, q, zero), jnp.where(lo, zero, q)], axis=0)


def _scores(q2, kb):
    return lax.dot_general(q2, kb, (((1,), (1,)), ((), ())), preferred_element_type=F32)


def _finish_heads(acc, l, lam, gout, rows, lam0):
    o = acc[:rows] / l[:rows] - lam * (acc[rows:] / l[rows:])
    return _rms(o, gout) * (1.0 - lam0)


def _attn_prompt_kernel(q_ref, k_ref, v_ref, lq1_ref, lk1_ref, lq2_ref, lk2_ref, gout_ref, o_init_ref,
                        o_ref, q2_sc, m_sc, acc_sc, *, bq, lam0):
    del o_init_ref
    i = pl.program_id(1)
    bk = bq
    q2_sc[...] = _split_q(q_ref[...])
    m_sc[...] = jnp.full(m_sc.shape, NEG_INF, F32)
    acc_sc[...] = jnp.zeros(acc_sc.shape, F32)
    ones = jnp.ones((bk, LANES), BF16)

    def update(off, masked):
        kb = k_ref[pl.ds(off, bk), :]
        vb = v_ref[pl.ds(off, bk), :]
        s = _scores(q2_sc[...], kb)
        if masked:
            q_chunk = (lax.broadcasted_iota(jnp.int32, s.shape, 0) % bq) // CHUNK
            k_chunk = lax.broadcasted_iota(jnp.int32, s.shape, 1) // CHUNK
            s = jnp.where(k_chunk <= q_chunk, s, NEG_INF)
        m_old = m_sc[...]
        m_new = jnp.maximum(m_old, jnp.max(s, axis=-1, keepdims=True))
        pr = jnp.exp(s - jnp.concatenate([m_new] * (bk // LANES), axis=1))
        alpha = jnp.exp(m_old - m_new)
        pv = jnp.dot(pr.astype(BF16), jnp.concatenate([vb, ones], axis=1), preferred_element_type=F32)
        acc_sc[...] = jnp.concatenate([alpha, alpha], axis=1) * acc_sc[...] + pv
        m_sc[...] = m_new

    def body(j, carry):
        update(pl.multiple_of(j * bk, bk), False)
        return carry

    lax.fori_loop(0, i, body, 0)
    update(pl.multiple_of(i * bk, bk), True)
    lam = _lambda_value(lq1_ref, lk1_ref, lq2_ref, lk2_ref, lam0)
    acc = acc_sc[...]
    o_ref[...] = _finish_heads(acc[:, :LANES], acc[:, LANES:], lam, gout_ref[...], bq, lam0).astype(o_ref.dtype)


def _attn_prompt(q, kb, vb, lam_params, o_init, *, layer, n_prompt, lam0):
    bq = _row_tile(n_prompt, 1024, CHUNK)
    return pl.pallas_call(
        functools.partial(_attn_prompt_kernel, bq=bq, lam0=lam0),
        grid=(N_HEADS, n_prompt // bq),
        in_specs=[
            pl.BlockSpec((bq, LANES), lambda h, i: (i, h)),
            pl.BlockSpec((n_prompt, LANES), lambda h, i: (0, h)),
            pl.BlockSpec((n_prompt, LANES), lambda h, i: (0, h)),
            _layer_vec(layer, HEAD_DIM), _layer_vec(layer, HEAD_DIM), _layer_vec(layer, HEAD_DIM),
            _layer_vec(layer, HEAD_DIM), _layer_vec(layer, V_DIM),
            pl.BlockSpec(memory_space=pl.ANY),
        ],
        out_specs=pl.BlockSpec((bq, LANES), lambda h, i: (i, h)),
        out_shape=jax.ShapeDtypeStruct(o_init.shape, o_init.dtype),
        scratch_shapes=[
            pltpu.VMEM((2 * bq, LANES), BF16),
            pltpu.VMEM((2 * bq, LANES), F32),
            pltpu.VMEM((2 * bq, 2 * LANES), F32),
        ],
        input_output_aliases={8: 0},
        compiler_params=_params(("parallel", "arbitrary")),
        name="attn_prompt",
    )(q, kb, vb, *lam_params, o_init)


def _attn_sample_kernel(q_ref, k_ref, v_ref, ck_ref, cv_ref, lq1_ref, lk1_ref, lq2_ref, lk2_ref, gout_ref,
                        o_init_ref, o_ref, *, rows, lam0):
    del o_init_ref
    lam = _lambda_value(lq1_ref, lk1_ref, lq2_ref, lk2_ref, lam0)
    for h in range(N_HEADS):
        sl = slice(h * LANES, (h + 1) * LANES)
        q2 = _split_q(q_ref[:, sl])
        s_c = _scores(q2, ck_ref[:, h, :].astype(BF16))
        s_n = _scores(q2, k_ref[:, sl])
        m = jnp.maximum(jnp.max(s_c, axis=-1, keepdims=True), jnp.max(s_n, axis=-1, keepdims=True))
        p_c = jnp.exp(s_c - m)
        p_n = jnp.exp(s_n - m)
        l = jnp.sum(p_c, axis=-1, keepdims=True) + jnp.sum(p_n, axis=-1, keepdims=True)
        acc = (jnp.dot(p_c.astype(BF16), cv_ref[:, h, :].astype(BF16), preferred_element_type=F32)
               + jnp.dot(p_n.astype(BF16), v_ref[:, sl], preferred_element_type=F32))
        o_ref[:, sl] = _finish_heads(acc, l, lam, gout_ref[...], rows, lam0).astype(o_ref.dtype)


def _attn_sample(q, kb, vb, cache_k, cache_v, lam_params, o_init, *, layer, n_prompt, rows, lam0):
    n_streams, past = cache_k.shape[1], cache_k.shape[2]
    base = n_prompt // rows
    row_spec = pl.BlockSpec((rows, D_ATTN), lambda b: (base + b, 0))
    cache_spec = pl.BlockSpec((None, None, past, N_HEADS, LANES), lambda b: (layer, b, 0, 0, 0))
    return pl.pallas_call(
        functools.partial(_attn_sample_kernel, rows=rows, lam0=lam0),
        grid=(n_streams,),
        in_specs=[row_spec, row_spec, row_spec, cache_spec, cache_spec,
                  _layer_vec(layer, HEAD_DIM), _layer_vec(layer, HEAD_DIM), _layer_vec(layer, HEAD_DIM),
                  _layer_vec(layer, HEAD_DIM), _layer_vec(layer, V_DIM),
                  pl.BlockSpec(memory_space=pl.ANY)],
        out_specs=row_spec,
        out_shape=jax.ShapeDtypeStruct(o_init.shape, o_init.dtype),
        input_output_aliases={10: 0},
        compiler_params=_params(("parallel",)),
        name="attn_sample",
    )(q, kb, vb, cache_k, cache_v, *lam_params, o_init)


def _gelu_tanh(x):
    return 0.5 * x * (1.0 + jnp.tanh(math.sqrt(2.0 / math.pi) * (x + 0.044715 * (x * x * x))))


def _rnn_kernel(xr_ref, gate_ref, h0_ref, cbuf_ref, cw_ref, cb_ref, wa_ref, ba_ref, wx_ref, bx_ref,
                lam_ref, gn_ref, y_init_ref, y_ref, hlast_ref, ctail_ref, h_sc, tail_sc, xc_sc, *, tb):
    del y_init_ref
    t = pl.program_id(1)

    @pl.when(t == 0)
    def _():
        h_sc[...] = h0_ref[...]
        tail_sc[...] = cbuf_ref[...]

    x = xr_ref[...]
    cw = cw_ref[...]
    cb = cb_ref[...]

    def conv(window, rows):
        acc = cb + window * cw[CONV_W - 1:CONV_W]
        for back in range(1, CONV_W):
            acc = acc + pltpu.roll(window, back, axis=0) * cw[CONV_W - 1 - back:CONV_W - back]
        return acc[-rows:]

    xc_sc[...] = conv(x, tb)
    head = jnp.concatenate([tail_sc[...], x[:SUBLANES]], axis=0)
    xc_sc[0:SUBLANES, :] = conv(head, SUBLANES)
    tail_sc[...] = x[tb - SUBLANES:]
    ctail_ref[...] = x[tb - SUBLANES:]
    xc = xc_sc[...]

    xb = xc.astype(BF16)
    r_parts, i_parts = [], []
    for n in range(N_RNN_BLOCKS):
        sl = slice(n * LANES, (n + 1) * LANES)
        r_parts.append(jnp.dot(xb[:, sl], wa_ref[n].astype(BF16), preferred_element_type=F32))
        i_parts.append(jnp.dot(xb[:, sl], wx_ref[n].astype(BF16), preferred_element_type=F32))
    r = jax.nn.sigmoid(jnp.concatenate(r_parts, axis=-1) + ba_ref[...])
    ig = jax.nn.sigmoid(jnp.concatenate(i_parts, axis=-1) + bx_ref[...])
    neg_lam = -lam_ref[...]
    softplus = jnp.maximum(neg_lam, 0.0) + jnp.log1p(jnp.exp(-jnp.abs(neg_lam)))
    log_a = (-RG_C * r) * softplus
    a = jnp.exp(log_a)
    u = jnp.sqrt(1.0 - a * a) * (ig * xc)

    row = lax.broadcasted_iota(jnp.int32, (tb, 1), 0)
    d = 1
    while d < tb:
        keep = row >= d
        u = jnp.where(keep, a * pltpu.roll(u, d, axis=0) + u, u)
        a = jnp.where(keep, a * pltpu.roll(a, d, axis=0), a)
        d *= 2
    hs = u + a * h_sc[...]
    h_sc[...] = hs[tb - 1:tb]
    hlast_ref[...] = hs[tb - 1:tb]

    y_ref[...] = _rms(hs * _gelu_tanh(gate_ref[...]), gn_ref[...]).astype(y_ref.dtype)


def _rnn(xr, gate, h0, cbuf, weights, y_init, *, layer, state_layer, row0, n_seq, seq_len, tb):
    C = xr.shape[1]
    nb = seq_len // tb
    base = row0 // tb
    row_spec = pl.BlockSpec((tb, C), lambda b, t: (base + b * nb + t, 0))
    blk = lambda: pl.BlockSpec((None, N_RNN_BLOCKS, LANES, LANES), lambda b, t: (layer, 0, 0, 0))
    return pl.pallas_call(
        functools.partial(_rnn_kernel, tb=tb),
        grid=(n_seq, nb),
        in_specs=[row_spec, row_spec,
                  pl.BlockSpec((None, None, 1, C), lambda b, t: (state_layer, b, 0, 0)),
                  pl.BlockSpec((None, None, SUBLANES, C), lambda b, t: (state_layer, b, 0, 0)),
                  pl.BlockSpec((None, CONV_W, C), lambda b, t: (layer, 0, 0)), _layer_vec(layer, C),
                  blk(), _layer_vec(layer, C), blk(), _layer_vec(layer, C), _layer_vec(layer, C),
                  _layer_vec(layer, C),
                  pl.BlockSpec(memory_space=pl.ANY)],
        out_specs=[row_spec,
                   pl.BlockSpec((None, 1, C), lambda b, t: (b, 0, 0)),
                   pl.BlockSpec((None, SUBLANES, C), lambda b, t: (b, 0, 0))],
        out_shape=[jax.ShapeDtypeStruct(y_init.shape, y_init.dtype),
                   jax.ShapeDtypeStruct((n_seq, 1, C), F32),
                   jax.ShapeDtypeStruct((n_seq, SUBLANES, C), F32)],
        scratch_shapes=[pltpu.VMEM((1, C), F32), pltpu.VMEM((SUBLANES, C), F32), pltpu.VMEM((tb, C), F32)],
        input_output_aliases={12: 0},
        compiler_params=_params(("arbitrary", "arbitrary")),
        name="rnn",
    )(xr, gate, h0, cbuf, *weights, y_init)


def _out_proj_kernel(o_ref, y_ref, wa_ref, wb_ref, x_ref, out_ref):
    acc = jnp.dot(o_ref[...], wa_ref[...].astype(BF16), preferred_element_type=F32)
    acc = acc + jnp.dot(y_ref[...], wb_ref[...].astype(BF16), preferred_element_type=F32)
    out_ref[...] = x_ref[...] + acc


def _out_proj(o, y, w_out, x, *, layer):
    T, D = x.shape
    half = o.shape[1]
    tm = _row_tile(T, 768)
    tn = 512
    return pl.pallas_call(
        _out_proj_kernel,
        grid=(T // tm, D // tn),
        in_specs=[
            pl.BlockSpec((tm, half), lambda i, j: (i, 0)),
            pl.BlockSpec((tm, half), lambda i, j: (i, 0)),
            pl.BlockSpec((None, half, tn), lambda i, j: (layer, 0, j)),
            pl.BlockSpec((None, half, tn), lambda i, j: (layer, 1, j)),
            pl.BlockSpec((tm, tn), lambda i, j: (i, j)),
        ],
        out_specs=pl.BlockSpec((tm, tn), lambda i, j: (i, j)),
        out_shape=jax.ShapeDtypeStruct((T, D), F32),
        compiler_params=_params(("parallel", "arbitrary")),
        name="out_proj",
    )(o, y, w_out, w_out, x)


def _router_kernel(x_ref, g_ref, w_ref, b_ref, hn_ref, logit_ref):
    hn = _rms(x_ref[...], g_ref[...])
    hn_ref[...] = hn
    logit_ref[...] = jnp.dot(hn.astype(BF16), w_ref[...].astype(BF16), preferred_element_type=F32) + b_ref[...]


def _router(x, g, w, b, *, layer):
    T, D = x.shape
    tm = _row_tile(T, 768)
    return pl.pallas_call(
        _router_kernel,
        grid=(T // tm,),
        in_specs=[pl.BlockSpec((tm, D), lambda i: (i, 0)),
                  _layer_vec(layer, D),
                  pl.BlockSpec((None, D, ROUTER_LANES), lambda i: (layer, 0, 0)),
                  _layer_vec(layer, ROUTER_LANES)],
        out_specs=[pl.BlockSpec((tm, D), lambda i: (i, 0)),
                   pl.BlockSpec((tm, ROUTER_LANES), lambda i: (i, 0))],
        out_shape=[jax.ShapeDtypeStruct((T, D), F32), jax.ShapeDtypeStruct((T, ROUTER_LANES), F32)],
        compiler_params=_params(("parallel",)),
        name="router",
    )(x, g, w, b)


def _top1(x):
    val = jnp.max(x, axis=-1, keepdims=True)
    n = x.shape[-1]
    idx = jnp.min(jnp.where(x == val, lax.broadcasted_iota(jnp.int32, x.shape, x.ndim - 1), n), axis=-1,
                  keepdims=True)
    return val, idx


def _routing_tables(logits, n_tokens):
    T = n_tokens
    gp = jax.nn.softmax(logits[:, :N_GROUPS], axis=-1)
    g_val, g_idx = _top1(gp)
    el = logits[:, N_GROUPS:N_GROUPS + N_EXPERTS].reshape(T, N_GROUPS, EXPERTS_PER_GROUP)
    group = lax.broadcasted_iota(jnp.int32, el.shape, 1)
    el_g = jnp.sum(jnp.where(group == g_idx[:, :, None], el, 0.0), axis=1)
    v0, i0 = _top1(el_g)
    lane = lax.broadcasted_iota(jnp.int32, el_g.shape, 1)
    v1, i1 = _top1(jnp.where(lane == i0, -jnp.inf, el_g))
    e_val = jnp.concatenate([v0, v1], axis=-1)
    e_idx = jnp.concatenate([i0, i1], axis=-1)
    wts = g_val * jax.nn.softmax(e_val, axis=-1)
    eid = (g_idx * EXPERTS_PER_GROUP + e_idx).astype(jnp.int32)

    A = T * TOP_K
    B = MOE_ROWS
    n_blocks = -(-(A + N_EXPERTS * (B - 1)) // B)
    n_rows = n_blocks * B
    eid_f = eid.reshape(A)
    onehot = (eid_f[:, None] == jnp.arange(N_EXPERTS, dtype=jnp.int32)[None, :]).astype(jnp.int32)
    csum = jnp.cumsum(onehot, axis=0)
    rank = jnp.sum(onehot * csum, axis=1) - 1
    counts = csum[-1]
    padded = (counts + B - 1) // B * B
    pad_end = jnp.cumsum(padded)
    pad_start = pad_end - padded
    dest = jnp.sum(onehot * pad_start[None, :], axis=1) + rank
    row_a = jnp.full((n_rows,), -1, jnp.int32).at[dest].set(jnp.arange(A, dtype=jnp.int32))
    tok = jnp.maximum(row_a, 0) // TOP_K
    row_src = tok
    rows = jnp.arange(n_rows, dtype=jnp.int32)
    dump = TOP_K * T + ((rows // B) % 2) * B + rows % B
    row_dst = jnp.where(row_a < 0, dump, (row_a % TOP_K) * T + tok)
    blk_exp = jnp.minimum(
        jnp.sum((pad_end[None, :] <= (jnp.arange(n_blocks, dtype=jnp.int32) * B)[:, None]).astype(jnp.int32), axis=1),
        N_EXPERTS - 1).astype(jnp.int32)
    n_used = (pad_end[-1] // B).astype(jnp.int32).reshape(1)
    return blk_exp, row_src, row_dst, n_used, wts


def _moe_kernel(blk_ref, src_ref, dst_ref, nused_ref, hn_ref, w1_ref, w3_ref, w2_ref,
                y_ref, xg_sc, yo_sc, gsem, ssem, *, n_blocks):
    del blk_ref
    B = MOE_ROWS
    n_real = y_ref.shape[0] - 2 * B
    i = pl.program_id(0)
    n_used = nused_ref[0]
    slot = i % 2

    def start_gather(blk, s):
        def body(r, c):
            src = src_ref[blk * B + r]
            pltpu.make_async_copy(hn_ref.at[pl.ds(src, 1)], xg_sc.at[s, pl.ds(r, 1)], gsem.at[s]).start()
            return c
        lax.fori_loop(0, B, body, 0, unroll=8)

    def wait_gather(s):
        pltpu.make_async_copy(hn_ref.at[pl.ds(0, B)], xg_sc.at[s], gsem.at[s]).wait()

    def start_scatter(blk, s):
        def body(r, c):
            dst = dst_ref[blk * B + r]
            pltpu.make_async_copy(yo_sc.at[s, pl.ds(r, 1)], y_ref.at[pl.ds(dst, 1)], ssem.at[s]).start()
            return c
        lax.fori_loop(0, B, body, 0, unroll=8)

    def wait_scatter(s):
        pltpu.make_async_copy(yo_sc.at[s], y_ref.at[pl.ds(0, B)], ssem.at[s]).wait()

    @pl.when(i == 0)
    def _():
        start_gather(0, 0)
        yo_sc[0] = jnp.zeros(yo_sc.shape[1:], F32)
        for half in range(2):
            fill = pltpu.make_async_copy(yo_sc.at[0], y_ref.at[pl.ds(n_real + half * B, B)], ssem.at[0])
            fill.start()
            fill.wait()

    @pl.when(i + 1 < n_used)
    def _():
        start_gather(i + 1, 1 - slot)

    @pl.when(i < n_used)
    def _():
        wait_gather(slot)

        @pl.when(i >= 2)
        def _():
            wait_scatter(slot)

        xb = xg_sc[slot].astype(BF16)
        h1 = jnp.dot(xb, w1_ref[...].astype(BF16), preferred_element_type=F32)
        h3 = jnp.dot(xb, w3_ref[...].astype(BF16), preferred_element_type=F32)
        h = (h1 * jax.nn.sigmoid(h1) * h3).astype(BF16)
        yo_sc[slot] = jnp.dot(h, w2_ref[...].astype(BF16), preferred_element_type=F32)
        start_scatter(i, slot)

    @pl.when(i == n_blocks - 1)
    def _():
        wait_scatter((n_used - 1) % 2)

        @pl.when(n_used >= 2)
        def _():
            wait_scatter(n_used % 2)


def _moe(hn, blk_exp, row_src, row_dst, n_used, w1, w3, w2, *, layer):
    T, D = hn.shape
    B = MOE_ROWS
    n_blocks = blk_exp.shape[0]
    de = w1.shape[3]
    grid_spec = pltpu.PrefetchScalarGridSpec(
        num_scalar_prefetch=4,
        grid=(n_blocks,),
        in_specs=[
            pl.BlockSpec(memory_space=pl.ANY),
            pl.BlockSpec((None, None, D, de), lambda i, blk, src, dst, nu: (layer, blk[i], 0, 0)),
            pl.BlockSpec((None, None, D, de), lambda i, blk, src, dst, nu: (layer, blk[i], 0, 0)),
            pl.BlockSpec((None, None, de, D), lambda i, blk, src, dst, nu: (layer, blk[i], 0, 0)),
        ],
        out_specs=pl.BlockSpec(memory_space=pl.ANY),
        scratch_shapes=[
            pltpu.VMEM((2, B, D), F32),
            pltpu.VMEM((2, B, D), F32),
            pltpu.SemaphoreType.DMA((2,)),
            pltpu.SemaphoreType.DMA((2,)),
        ],
    )
    return pl.pallas_call(
        functools.partial(_moe_kernel, n_blocks=n_blocks),
        grid_spec=grid_spec,
        out_shape=jax.ShapeDtypeStruct((TOP_K * T + 2 * B, D), F32),
        compiler_params=_params(("arbitrary",)),
        name="moe",
    )(blk_exp, row_src, row_dst, n_used, hn, w1, w3, w2)


def _ple_kernel(x_ref, y0_ref, y1_ref, wt_ref, g_ref, p_ref, wg_ref, wp_ref, out_ref, x2_sc, hn_sc, *, tn):
    j = pl.program_id(1)

    @pl.when(j == 0)
    def _():
        wt = wt_ref[...]
        x2 = x_ref[...] + (wt[:, 0:1] * y0_ref[...] + wt[:, 1:2] * y1_ref[...])
        hn_sc[...] = _rms(x2, g_ref[...]).astype(BF16)
        for c in range(x2_sc.shape[0]):
            x2_sc[c] = x2[:, c * tn:(c + 1) * tn]

    gate = jax.nn.sigmoid(jnp.dot(hn_sc[...], wg_ref[...].astype(BF16), preferred_element_type=F32))
    proj = jnp.dot(p_ref[...].astype(BF16), wp_ref[...].astype(BF16), preferred_element_type=F32)
    out_ref[...] = x2_sc[j] + gate * proj


def _ple(x, y_slots, wts, g, p, wg, wp, *, layer):
    T, D = x.shape
    dp = p.shape[2]
    tm = _row_tile(T, 384)
    tn = 512
    nt = T // tm
    return pl.pallas_call(
        functools.partial(_ple_kernel, tn=tn),
        grid=(nt, D // tn),
        in_specs=[
            pl.BlockSpec((tm, D), lambda i, j: (i, 0)),
            pl.BlockSpec((tm, D), lambda i, j: (i, 0)),
            pl.BlockSpec((tm, D), lambda i, j: (i + nt, 0)),
            pl.BlockSpec((tm, TOP_K), lambda i, j: (i, 0)),
            _layer_vec(layer, D),
            pl.BlockSpec((None, tm, dp), lambda i, j: (layer, i, 0)),
            pl.BlockSpec((None, D, tn), lambda i, j: (layer, 0, j)),
            pl.BlockSpec((None, dp, tn), lambda i, j: (layer, 0, j)),
        ],
        out_specs=pl.BlockSpec((tm, tn), lambda i, j: (i, j)),
        out_shape=jax.ShapeDtypeStruct((T, D), F32),
        scratch_shapes=[pltpu.VMEM((D // tn, tm, tn), F32), pltpu.VMEM((tm, D), BF16)],
        compiler_params=_params(("parallel", "arbitrary")),
        name="ple",
    )(x, y_slots, y_slots, wts, g, p, wg, wp)


def kernel(x_prompt, x_sample, cache_k, cache_v, state_rnn_h, state_conv, p_prompt, p_sample, norm_mix, w_in, q_norm, k_norm, lambda_q1, lambda_k1, lambda_q2, lambda_k2, attn_out_norm, conv_w, conv_b, w_rg_a, b_rg_a, w_rg_x, b_rg_x, rg_lambda, rnn_out_norm, w_out, norm_ffn, w_router_group, b_router_group, w_router_expert, b_router_expert, w_exp_gate, w_exp_up, w_exp_down, norm_ple, w_ple_gate, w_ple_proj):
    bp, seq, D = x_prompt.shape
    nb, dseq, _ = x_sample.shape
    depth = w_in.shape[0]
    d_rnn = state_rnn_h.shape[-1]
    d_ple = p_prompt.shape[-1]
    assert bp == 1 and seq % CHUNK == 0 and dseq % 16 == 0 and seq % dseq == 0
    n_p = bp * seq
    n_s = nb * dseq
    T = n_p + n_s

    x = jnp.concatenate([x_prompt.reshape(n_p, D), x_sample.reshape(n_s, D)], axis=0)
    p_all = jnp.concatenate([p_prompt.reshape(depth, n_p, d_ple), p_sample.reshape(depth, n_s, d_ple)], axis=1)
    vec = lambda a: a.reshape(depth, 1, -1)
    conv_pad = jnp.pad(state_conv, ((0, 0), (0, 0), (SUBLANES - (CONV_W - 1), 0), (0, 0)))
    h0_s = state_rnn_h.reshape(depth, nb, 1, d_rnn)
    zero_h = jnp.zeros((1, bp, 1, d_rnn), F32)
    zero_conv = jnp.zeros((1, bp, SUBLANES, d_rnn), F32)
    pad_lanes = ROUTER_LANES - N_GROUPS - N_EXPERTS
    w_router = jnp.concatenate([w_router_group, w_router_expert, jnp.zeros((depth, D, pad_lanes), F32)], axis=-1)
    b_router = jnp.concatenate([b_router_group, b_router_expert, jnp.zeros((depth, pad_lanes), F32)], axis=-1)
    tb_p = _row_tile(seq, 256, SUBLANES)

    norm_mix_v, qg_v, kg_v = vec(norm_mix), vec(q_norm), vec(k_norm)
    lam_params = (vec(lambda_q1), vec(lambda_k1), vec(lambda_q2), vec(lambda_k2), vec(attn_out_norm))
    rnn_w = (conv_w, vec(conv_b), w_rg_a, vec(b_rg_a), w_rg_x, vec(b_rg_x), vec(rg_lambda), vec(rnn_out_norm))
    norm_ffn_v, b_router_v, norm_ple_v = vec(norm_ffn), vec(b_router), vec(norm_ple)

    ks, vs = [], []
    hp, cp, hs_, cs_ = [], [], [], []
    for l in range(depth):
        lam0 = _lambda_init(l)
        q, k, v, xr, gate, kb, vb = _in_proj(x, norm_mix_v, w_in, qg_v, kg_v, layer=l)
        ks.append(k)
        vs.append(v)
        o = jnp.zeros((T, D_ATTN), BF16)
        o = _attn_prompt(q, kb, vb, lam_params, o, layer=l, n_prompt=n_p, lam0=lam0)
        o = _attn_sample(q, kb, vb, cache_k, cache_v, lam_params, o, layer=l, n_prompt=n_p, rows=dseq, lam0=lam0)

        y = jnp.zeros((T, d_rnn), BF16)
        y, h_p, c_p = _rnn(xr, gate, zero_h, zero_conv, rnn_w, y, layer=l, state_layer=0,
                           row0=0, n_seq=bp, seq_len=seq, tb=tb_p)
        y, h_s, c_s = _rnn(xr, gate, h0_s, conv_pad, rnn_w, y, layer=l, state_layer=l,
                           row0=n_p, n_seq=nb, seq_len=dseq, tb=dseq)

        x = _out_proj(o, y, w_out, x, layer=l)

        hn, logits = _router(x, norm_ffn_v, w_router, b_router_v, layer=l)
        blk_exp, row_src, row_dst, n_used, wts = _routing_tables(logits, T)
        y_slots = _moe(hn, blk_exp, row_src, row_dst, n_used, w_exp_gate, w_exp_up, w_exp_down, layer=l)

        x = _ple(x, y_slots, wts, norm_ple_v, p_all, w_ple_gate, w_ple_proj, layer=l)

        hp.append(h_p[:, 0])
        cp.append(c_p[:, SUBLANES - (CONV_W - 1):])
        hs_.append(h_s[:, 0])
        cs_.append(c_s[:, SUBLANES - (CONV_W - 1):])

    k_p, k_s, v_p, v_s = _kv_layout(ks, vs, n_prompt=n_p, n_sample=n_s)
    return (x[:n_p].reshape(bp, seq, D), x[n_p:].reshape(nb, dseq, D),
            k_p.reshape(depth, bp, seq, N_HEADS, QK_DIM), v_p.reshape(depth, bp, seq, N_HEADS, V_DIM),
            jnp.stack(hp), jnp.stack(cp),
            k_s.reshape(depth, nb, dseq, N_HEADS, QK_DIM), v_s.reshape(depth, nb, dseq, N_HEADS, V_DIM),
            jnp.stack(hs_), jnp.stack(cs_))
```

```python
import functools
import math

import jax
import jax.numpy as jnp
from jax import lax
from jax.experimental import pallas as pl
from jax.experimental.pallas import tpu as pltpu

F32 = jnp.float32
BF16 = jnp.bfloat16

CHUNK = 64
N_HEADS = 8
HEAD_DIM = 64
QK_DIM = 2 * HEAD_DIM
V_DIM = 2 * HEAD_DIM
D_ATTN = N_HEADS * V_DIM
N_RNN_BLOCKS = 8
CONV_W = 4
RG_C = 8.0
N_GROUPS = 4
EXPERTS_PER_GROUP = 8
N_EXPERTS = N_GROUPS * EXPERTS_PER_GROUP
TOP_K = 2
EPS = 1e-6
NEG_INF = -1e30

LANES = 128
SUBLANES = 8
VMEM_LIMIT_BYTES = 56 * 1024 * 1024
MOE_ROWS = 256
ROUTER_LANES = 128
ROW_DMA_UNROLL = 8


def _lambda_init(layer):
    return 0.8 - 0.6 * math.exp(-0.3 * layer)


def _row_tile(total, target, multiple=16):
    best = None
    for t in range(multiple, min(total, target) + 1, multiple):
        if total % t == 0:
            best = t
    assert best is not None, (total, target)
    return best


def _params(sem):
    return pltpu.CompilerParams(dimension_semantics=sem, vmem_limit_bytes=VMEM_LIMIT_BYTES)


def _layer_vec(l, n):
    return pl.BlockSpec((None, 1, n), lambda *_: (l, 0, 0))


def _rms(x, gain):
    ms = jnp.mean(x * x, axis=-1, keepdims=True)
    return x * lax.rsqrt(ms + EPS) * gain


def _half_norm(a, gain, lo):
    sq = a * a
    s_lo = jnp.sum(jnp.where(lo, sq, 0.0), axis=-1, keepdims=True)
    s_hi = jnp.sum(jnp.where(lo, 0.0, sq), axis=-1, keepdims=True)
    ms = jnp.where(lo, s_lo, s_hi) * (1.0 / HEAD_DIM)
    return a * lax.rsqrt(ms + EPS) * gain


def _in_proj_kernel(x_ref, g_ref, w_ref, qg_ref, kg_ref,
                    q_ref, k_ref, v_ref, xr_ref, gate_ref, kb_ref, vb_ref, hn_ref, *, steps_per_section, tn):
    j = pl.program_id(1)

    @pl.when(j == 0)
    def _():
        hn_ref[...] = _rms(x_ref[...], g_ref[...]).astype(BF16)

    acc = jnp.dot(hn_ref[...], w_ref[...].astype(BF16), preferred_element_type=F32)
    sec = j // steps_per_section
    lo = lax.broadcasted_iota(jnp.int32, (1, LANES), 1) < HEAD_DIM

    @pl.when(sec == 0)
    def _():
        for h in range(tn // LANES):
            sl = slice(h * LANES, (h + 1) * LANES)
            q_ref[:, sl] = (_half_norm(acc[:, sl], qg_ref[...], lo) * (HEAD_DIM ** -0.5)).astype(BF16)

    @pl.when(sec == 1)
    def _():
        for h in range(tn // LANES):
            sl = slice(h * LANES, (h + 1) * LANES)
            kn = _half_norm(acc[:, sl], kg_ref[...], lo)
            k_ref[:, sl] = kn
            kb_ref[:, sl] = kn.astype(BF16)

    @pl.when(sec == 2)
    def _():
        v_ref[...] = acc
        vb_ref[...] = acc.astype(BF16)

    @pl.when(sec == 3)
    def _():
        xr_ref[...] = acc

    @pl.when(sec == 4)
    def _():
        gate_ref[...] = acc


def _in_proj(x, g, w_in, qg, kg, *, layer):
    T, D = x.shape
    tm = _row_tile(T, 768)
    tn = 512
    sec_w = D_ATTN
    sps = sec_w // tn
    n_steps = w_in.shape[2] // tn
    col = lambda sec: (lambda j: jnp.clip(j - sec * sps, 0, sps - 1))

    def out_spec(sec):
        return pl.BlockSpec((tm, tn), lambda i, j: (i, col(sec)(j)))

    flat = jax.ShapeDtypeStruct((T, sec_w), F32)
    flat_b = jax.ShapeDtypeStruct((T, sec_w), BF16)
    return pl.pallas_call(
        functools.partial(_in_proj_kernel, steps_per_section=sps, tn=tn),
        grid=(T // tm, n_steps),
        in_specs=[
            pl.BlockSpec((tm, D), lambda i, j: (i, 0)),
            _layer_vec(layer, D),
            pl.BlockSpec((None, D, tn), lambda i, j: (layer, 0, j)),
            _layer_vec(layer, LANES),
            _layer_vec(layer, LANES),
        ],
        out_specs=[out_spec(s) for s in (0, 1, 2, 3, 4, 1, 2)],
        out_shape=[flat_b, flat, flat, flat, flat, flat_b, flat_b],
        scratch_shapes=[pltpu.VMEM((tm, D), BF16)],
        compiler_params=_params(("parallel", "arbitrary")),
        name="in_proj",
    )(x, g, w_in, qg, kg)


def _kv_layout_kernel(*refs, depth, prompt_tiles):
    srcs = refs[:2 * depth]
    kp_ref, ks_ref, vp_ref, vs_ref = refs[2 * depth:]
    l = pl.program_id(0)
    i = pl.program_id(1)

    def spread(src, dst):
        for h in range(N_HEADS):
            dst[:, h, :] = src[:, h * LANES:(h + 1) * LANES]

    for d in range(depth):
        @pl.when(jnp.logical_and(l == d, i < prompt_tiles))
        def _():
            spread(srcs[d], kp_ref)
            spread(srcs[depth + d], vp_ref)

        @pl.when(jnp.logical_and(l == d, i >= prompt_tiles))
        def _():
            spread(srcs[d], ks_ref)
            spread(srcs[depth + d], vs_ref)


def _kv_layout(ks, vs, *, n_prompt, n_sample):
    depth = len(ks)
    T, width = ks[0].shape
    tr = math.gcd(n_prompt, n_sample)
    nt, np_t, ns_t = T // tr, n_prompt // tr, n_sample // tr

    def src_spec(d):
        return pl.BlockSpec((tr, width), lambda l, i: (jnp.where(l < d, 0, jnp.where(l > d, nt - 1, i)), 0))

    blk = (None, tr, N_HEADS, LANES)
    prompt_spec = pl.BlockSpec(blk, lambda l, i: (l, jnp.minimum(i, np_t - 1), 0, 0))
    sample_spec = pl.BlockSpec(blk, lambda l, i: (l, jnp.clip(i - np_t, 0, ns_t - 1), 0, 0))
    shape = lambda n: jax.ShapeDtypeStruct((depth, n, N_HEADS, LANES), F32)
    return pl.pallas_call(
        functools.partial(_kv_layout_kernel, depth=depth, prompt_tiles=np_t),
        grid=(depth, nt),
        in_specs=[src_spec(d) for d in range(depth)] * 2,
        out_specs=[prompt_spec, sample_spec, prompt_spec, sample_spec],
        out_shape=[shape(n_prompt), shape(n_sample), shape(n_prompt), shape(n_sample)],
        compiler_params=_params(("arbitrary", "arbitrary")),
        name="kv_layout",
    )(*ks, *vs)


def _lambda_value(lq1_ref, lk1_ref, lq2_ref, lk2_ref, lam0):
    s1 = jnp.sum(lq1_ref[...] * lk1_ref[...], axis=-1, keepdims=True)
    s2 = jnp.sum(lq2_ref[...] * lk2_ref[...], axis=-1, keepdims=True)
    return jnp.exp(s1) - jnp.exp(s2) + lam0


def _split_q(q):
    lo = lax.broadcasted_iota(jnp.int32, q.shape, 1) < HEAD_DIM
    zero = jnp.zeros_like(q)
    return jnp.concatenate([jnp.where(lo, q, zero), jnp.where(lo, zero, q)], axis=0)


def _scores(q2, kb):
    return lax.dot_general(q2, kb, (((1,), (1,)), ((), ())), preferred_element_type=F32)


def _finish_heads(acc, l, lam, gout, rows, lam0):
    o = acc[:rows] / l[:rows] - lam * (acc[rows:] / l[rows:])
    return _rms(o, gout) * (1.0 - lam0)


def _attn_prompt_kernel(q_ref, k_ref, v_ref, lq1_ref, lk1_ref, lq2_ref, lk2_ref, gout_ref, o_init_ref,
                        o_ref, q2_sc, m_sc, acc_sc, *, bq, lam0):
    del o_init_ref
    i = pl.program_id(1)
    bk = bq
    q2_sc[...] = _split_q(q_ref[...])
    m_sc[...] = jnp.full(m_sc.shape, NEG_INF, F32)
    acc_sc[...] = jnp.zeros(acc_sc.shape, F32)
    ones = jnp.ones((bk, LANES), BF16)

    def update(off, masked):
        kb = k_ref[pl.ds(off, bk), :]
        vb = v_ref[pl.ds(off, bk), :]
        s = _scores(q2_sc[...], kb)
        if masked:
            q_chunk = (lax.broadcasted_iota(jnp.int32, s.shape, 0) % bq) // CHUNK
            k_chunk = lax.broadcasted_iota(jnp.int32, s.shape, 1) // CHUNK
            s = jnp.where(k_chunk <= q_chunk, s, NEG_INF)
        m_old = m_sc[...]
        m_new = jnp.maximum(m_old, jnp.max(s, axis=-1, keepdims=True))
        pr = jnp.exp(s - jnp.concatenate([m_new] * (bk // LANES), axis=1))
        alpha = jnp.exp(m_old - m_new)
        pv = jnp.dot(pr.astype(BF16), jnp.concatenate([vb, ones], axis=1), preferred_element_type=F32)
        acc_sc[...] = jnp.concatenate([alpha, alpha], axis=1) * acc_sc[...] + pv
        m_sc[...] = m_new

    def body(j, carry):
        update(pl.multiple_of(j * bk, bk), False)
        return carry

    lax.fori_loop(0, i, body, 0)
    update(pl.multiple_of(i * bk, bk), True)
    lam = _lambda_value(lq1_ref, lk1_ref, lq2_ref, lk2_ref, lam0)
    acc = acc_sc[...]
    o_ref[...] = _finish_heads(acc[:, :LANES], acc[:, LANES:], lam, gout_ref[...], bq, lam0).astype(o_ref.dtype)


def _score_bound(qg_ref, kg_ref):
    gq = jnp.max(jnp.abs(qg_ref[...]), axis=-1, keepdims=True)
    gk = jnp.max(jnp.abs(kg_ref[...]), axis=-1, keepdims=True)
    return (HEAD_DIM ** 0.5) * gq * gk


def _attn_prompt_shift_kernel(q_ref, k_ref, v_ref, lq1_ref, lk1_ref, lq2_ref, lk2_ref, gout_ref, qg_ref, kg_ref,
                              o_init_ref, o_ref, q2_sc, acc_sc, *, bq, lam0):
    del o_init_ref
    i = pl.program_id(1)
    bk = bq
    q2_sc[...] = _split_q(q_ref[...])
    acc_sc[...] = jnp.zeros(acc_sc.shape, F32)
    ones = jnp.ones((bk, LANES), BF16)
    shift = _score_bound(qg_ref, kg_ref)

    def update(off, masked):
        kb = k_ref[pl.ds(off, bk), :]
        vb = v_ref[pl.ds(off, bk), :]
        s = _scores(q2_sc[...], kb) - shift
        if masked:
            q_chunk = (lax.broadcasted_iota(jnp.int32, s.shape, 0) % bq) // CHUNK
            k_chunk = lax.broadcasted_iota(jnp.int32, s.shape, 1) // CHUNK
            s = jnp.where(k_chunk <= q_chunk, s, NEG_INF)
        pr = jnp.exp(s).astype(BF16)
        acc_sc[...] += jnp.dot(pr, jnp.concatenate([vb, ones], axis=1), preferred_element_type=F32)

    def body(j, carry):
        update(pl.multiple_of(j * bk, bk), False)
        return carry

    lax.fori_loop(0, i, body, 0)
    update(pl.multiple_of(i * bk, bk), True)
    lam = _lambda_value(lq1_ref, lk1_ref, lq2_ref, lk2_ref, lam0)
    acc = acc_sc[...]
    o_ref[...] = _finish_heads(acc[:, :LANES], acc[:, LANES:], lam, gout_ref[...], bq, lam0).astype(o_ref.dtype)


MAX_FIXED_SHIFT = 30.0


def _attn_prompt(q, kb, vb, lam_params, qg, kg, o_init, *, layer, n_prompt, lam0):
    bq = _row_tile(n_prompt, 1024, CHUNK)
    common_in = [
        pl.BlockSpec((bq, LANES), lambda h, i: (i, h)),
        pl.BlockSpec((n_prompt, LANES), lambda h, i: (0, h)),
        pl.BlockSpec((n_prompt, LANES), lambda h, i: (0, h)),
        _layer_vec(layer, HEAD_DIM), _layer_vec(layer, HEAD_DIM), _layer_vec(layer, HEAD_DIM),
        _layer_vec(layer, HEAD_DIM), _layer_vec(layer, V_DIM),
    ]
    common = dict(
        grid=(N_HEADS, n_prompt // bq),
        out_specs=pl.BlockSpec((bq, LANES), lambda h, i: (i, h)),
        out_shape=jax.ShapeDtypeStruct(o_init.shape, o_init.dtype),
        compiler_params=_params(("parallel", "arbitrary")),
    )
    q2_scratch = pltpu.VMEM((2 * bq, LANES), BF16)
    acc_scratch = pltpu.VMEM((2 * bq, 2 * LANES), F32)

    def streaming(o):
        return pl.pallas_call(
            functools.partial(_attn_prompt_kernel, bq=bq, lam0=lam0),
            in_specs=common_in + [pl.BlockSpec(memory_space=pl.ANY)],
            scratch_shapes=[q2_scratch, pltpu.VMEM((2 * bq, LANES), F32), acc_scratch],
            input_output_aliases={8: 0},
            name="attn_prompt",
            **common,
        )(q, kb, vb, *lam_params, o)

    def fixed_shift(o):
        return pl.pallas_call(
            functools.partial(_attn_prompt_shift_kernel, bq=bq, lam0=lam0),
            in_specs=common_in + [_layer_vec(layer, LANES), _layer_vec(layer, LANES),
                                  pl.BlockSpec(memory_space=pl.ANY)],
            scratch_shapes=[q2_scratch, acc_scratch],
            input_output_aliases={10: 0},
            name="attn_prompt_shift",
            **common,
        )(q, kb, vb, *lam_params, qg, kg, o)

    bound = (HEAD_DIM ** 0.5) * jnp.max(jnp.abs(qg[layer])) * jnp.max(jnp.abs(kg[layer]))
    return lax.cond(bound <= MAX_FIXED_SHIFT, fixed_shift, streaming, o_init)


def _attn_sample_kernel(q_ref, k_ref, v_ref, ck_ref, cv_ref, lq1_ref, lk1_ref, lq2_ref, lk2_ref, gout_ref,
                        o_init_ref, o_ref, *, rows, lam0):
    del o_init_ref
    lam = _lambda_value(lq1_ref, lk1_ref, lq2_ref, lk2_ref, lam0)
    for h in range(N_HEADS):
        sl = slice(h * LANES, (h + 1) * LANES)
        q2 = _split_q(q_ref[:, sl])
        s_c = _scores(q2, ck_ref[:, h, :].astype(BF16))
        s_n = _scores(q2, k_ref[:, sl])
        m = jnp.maximum(jnp.max(s_c, axis=-1, keepdims=True), jnp.max(s_n, axis=-1, keepdims=True))
        p_c = jnp.exp(s_c - m)
        p_n = jnp.exp(s_n - m)
        l = jnp.sum(p_c, axis=-1, keepdims=True) + jnp.sum(p_n, axis=-1, keepdims=True)
        acc = (jnp.dot(p_c.astype(BF16), cv_ref[:, h, :].astype(BF16), preferred_element_type=F32)
               + jnp.dot(p_n.astype(BF16), v_ref[:, sl], preferred_element_type=F32))
        o_ref[:, sl] = _finish_heads(acc, l, lam, gout_ref[...], rows, lam0).astype(o_ref.dtype)


def _attn_sample(q, kb, vb, cache_k, cache_v, lam_params, o_init, *, layer, n_prompt, rows, lam0):
    n_streams, past = cache_k.shape[1], cache_k.shape[2]
    base = n_prompt // rows
    row_spec = pl.BlockSpec((rows, D_ATTN), lambda b: (base + b, 0))
    cache_spec = pl.BlockSpec((None, None, past, N_HEADS, LANES), lambda b: (layer, b, 0, 0, 0))
    return pl.pallas_call(
        functools.partial(_attn_sample_kernel, rows=rows, lam0=lam0),
        grid=(n_streams,),
        in_specs=[row_spec, row_spec, row_spec, cache_spec, cache_spec,
                  _layer_vec(layer, HEAD_DIM), _layer_vec(layer, HEAD_DIM), _layer_vec(layer, HEAD_DIM),
                  _layer_vec(layer, HEAD_DIM), _layer_vec(layer, V_DIM),
                  pl.BlockSpec(memory_space=pl.ANY)],
        out_specs=row_spec,
        out_shape=jax.ShapeDtypeStruct(o_init.shape, o_init.dtype),
        input_output_aliases={10: 0},
        compiler_params=_params(("parallel",)),
        name="attn_sample",
    )(q, kb, vb, cache_k, cache_v, *lam_params, o_init)


def _gelu_tanh(x):
    return 0.5 * x * (1.0 + jnp.tanh(math.sqrt(2.0 / math.pi) * (x + 0.044715 * (x * x * x))))


def _rnn_kernel(xr_ref, gate_ref, h0_ref, cbuf_ref, cw_ref, cb_ref, wa_ref, ba_ref, wx_ref, bx_ref,
                lam_ref, gn_ref, y_init_ref, y_ref, hlast_ref, ctail_ref, h_sc, tail_sc, xc_sc, *, tb):
    del y_init_ref
    t = pl.program_id(1)

    @pl.when(t == 0)
    def _():
        h_sc[...] = h0_ref[...]
        tail_sc[...] = cbuf_ref[...]

    x = xr_ref[...]
    cw = cw_ref[...]
    cb = cb_ref[...]

    def conv(window, rows):
        acc = cb + window * cw[CONV_W - 1:CONV_W]
        for back in range(1, CONV_W):
            acc = acc + pltpu.roll(window, back, axis=0) * cw[CONV_W - 1 - back:CONV_W - back]
        return acc[-rows:]

    xc_sc[...] = conv(x, tb)
    head = jnp.concatenate([tail_sc[...], x[:SUBLANES]], axis=0)
    xc_sc[0:SUBLANES, :] = conv(head, SUBLANES)
    tail_sc[...] = x[tb - SUBLANES:]
    ctail_ref[...] = x[tb - SUBLANES:]
    xc = xc_sc[...]

    xb = xc.astype(BF16)
    r_parts, i_parts = [], []
    for n in range(N_RNN_BLOCKS):
        sl = slice(n * LANES, (n + 1) * LANES)
        r_parts.append(jnp.dot(xb[:, sl], wa_ref[n].astype(BF16), preferred_element_type=F32))
        i_parts.append(jnp.dot(xb[:, sl], wx_ref[n].astype(BF16), preferred_element_type=F32))
    r = jax.nn.sigmoid(jnp.concatenate(r_parts, axis=-1) + ba_ref[...])
    ig = jax.nn.sigmoid(jnp.concatenate(i_parts, axis=-1) + bx_ref[...])
    neg_lam = -lam_ref[...]
    softplus = jnp.maximum(neg_lam, 0.0) + jnp.log1p(jnp.exp(-jnp.abs(neg_lam)))
    log_a = (-RG_C * r) * softplus
    a = jnp.exp(log_a)
    u = jnp.sqrt(1.0 - a * a) * (ig * xc)

    row = lax.broadcasted_iota(jnp.int32, (tb, 1), 0)
    d = 1
    while d < tb:
        keep = row >= d
        u = jnp.where(keep, a * pltpu.roll(u, d, axis=0) + u, u)
        a = jnp.where(keep, a * pltpu.roll(a, d, axis=0), a)
        d *= 2
    hs = u + a * h_sc[...]
    h_sc[...] = hs[tb - 1:tb]
    hlast_ref[...] = hs[tb - 1:tb]

    y_ref[...] = _rms(hs * _gelu_tanh(gate_ref[...]), gn_ref[...]).astype(y_ref.dtype)


def _rnn(xr, gate, h0, cbuf, weights, y_init, *, layer, state_layer, row0, n_seq, seq_len, tb):
    C = xr.shape[1]
    nb = seq_len // tb
    base = row0 // tb
    row_spec = pl.BlockSpec((tb, C), lambda b, t: (base + b * nb + t, 0))
    blk = lambda: pl.BlockSpec((None, N_RNN_BLOCKS, LANES, LANES), lambda b, t: (layer, 0, 0, 0))
    return pl.pallas_call(
        functools.partial(_rnn_kernel, tb=tb),
        grid=(n_seq, nb),
        in_specs=[row_spec, row_spec,
                  pl.BlockSpec((None, None, 1, C), lambda b, t: (state_layer, b, 0, 0)),
                  pl.BlockSpec((None, None, SUBLANES, C), lambda b, t: (state_layer, b, 0, 0)),
                  pl.BlockSpec((None, CONV_W, C), lambda b, t: (layer, 0, 0)), _layer_vec(layer, C),
                  blk(), _layer_vec(layer, C), blk(), _layer_vec(layer, C), _layer_vec(layer, C),
                  _layer_vec(layer, C),
                  pl.BlockSpec(memory_space=pl.ANY)],
        out_specs=[row_spec,
                   pl.BlockSpec((None, 1, C), lambda b, t: (b, 0, 0)),
                   pl.BlockSpec((None, SUBLANES, C), lambda b, t: (b, 0, 0))],
        out_shape=[jax.ShapeDtypeStruct(y_init.shape, y_init.dtype),
                   jax.ShapeDtypeStruct((n_seq, 1, C), F32),
                   jax.ShapeDtypeStruct((n_seq, SUBLANES, C), F32)],
        scratch_shapes=[pltpu.VMEM((1, C), F32), pltpu.VMEM((SUBLANES, C), F32), pltpu.VMEM((tb, C), F32)],
        input_output_aliases={12: 0},
        compiler_params=_params(("arbitrary", "arbitrary")),
        name="rnn",
    )(xr, gate, h0, cbuf, *weights, y_init)


def _out_proj_kernel(o_ref, y_ref, wa_ref, wb_ref, x_ref, out_ref):
    acc = jnp.dot(o_ref[...], wa_ref[...].astype(BF16), preferred_element_type=F32)
    acc = acc + jnp.dot(y_ref[...], wb_ref[...].astype(BF16), preferred_element_type=F32)
    out_ref[...] = x_ref[...] + acc


def _out_proj(o, y, w_out, x, *, layer):
    T, D = x.shape
    half = o.shape[1]
    tm = _row_tile(T, 768)
    tn = 512
    return pl.pallas_call(
        _out_proj_kernel,
        grid=(T // tm, D // tn),
        in_specs=[
            pl.BlockSpec((tm, half), lambda i, j: (i, 0)),
            pl.BlockSpec((tm, half), lambda i, j: (i, 0)),
            pl.BlockSpec((None, half, tn), lambda i, j: (layer, 0, j)),
            pl.BlockSpec((None, half, tn), lambda i, j: (layer, 1, j)),
            pl.BlockSpec((tm, tn), lambda i, j: (i, j)),
        ],
        out_specs=pl.BlockSpec((tm, tn), lambda i, j: (i, j)),
        out_shape=jax.ShapeDtypeStruct((T, D), F32),
        compiler_params=_params(("parallel", "arbitrary")),
        name="out_proj",
    )(o, y, w_out, w_out, x)


def _router_kernel(x_ref, g_ref, w_ref, b_ref, hn_ref, logit_ref):
    hn = _rms(x_ref[...], g_ref[...])
    hn_ref[...] = hn
    logit_ref[...] = jnp.dot(hn.astype(BF16), w_ref[...].astype(BF16), preferred_element_type=F32) + b_ref[...]


def _router(x, g, w, b, *, layer):
    T, D = x.shape
    tm = _row_tile(T, 768)
    return pl.pallas_call(
        _router_kernel,
        grid=(T // tm,),
        in_specs=[pl.BlockSpec((tm, D), lambda i: (i, 0)),
                  _layer_vec(layer, D),
                  pl.BlockSpec((None, D, ROUTER_LANES), lambda i: (layer, 0, 0)),
                  _layer_vec(layer, ROUTER_LANES)],
        out_specs=[pl.BlockSpec((tm, D), lambda i: (i, 0)),
                   pl.BlockSpec((tm, ROUTER_LANES), lambda i: (i, 0))],
        out_shape=[jax.ShapeDtypeStruct((T, D), F32), jax.ShapeDtypeStruct((T, ROUTER_LANES), F32)],
        compiler_params=_params(("parallel",)),
        name="router",
    )(x, g, w, b)


def _top1(x):
    val = jnp.max(x, axis=-1, keepdims=True)
    n = x.shape[-1]
    idx = jnp.min(jnp.where(x == val, lax.broadcasted_iota(jnp.int32, x.shape, x.ndim - 1), n), axis=-1,
                  keepdims=True)
    return val, idx


def _routing_tables(logits, n_tokens):
    T = n_tokens
    gp = jax.nn.softmax(logits[:, :N_GROUPS], axis=-1)
    g_val, g_idx = _top1(gp)
    el = logits[:, N_GROUPS:N_GROUPS + N_EXPERTS].reshape(T, N_GROUPS, EXPERTS_PER_GROUP)
    group = lax.broadcasted_iota(jnp.int32, el.shape, 1)
    el_g = jnp.sum(jnp.where(group == g_idx[:, :, None], el, 0.0), axis=1)
    v0, i0 = _top1(el_g)
    lane = lax.broadcasted_iota(jnp.int32, el_g.shape, 1)
    v1, i1 = _top1(jnp.where(lane == i0, -jnp.inf, el_g))
    e_val = jnp.concatenate([v0, v1], axis=-1)
    e_idx = jnp.concatenate([i0, i1], axis=-1)
    wts = g_val * jax.nn.softmax(e_val, axis=-1)
    eid = (g_idx * EXPERTS_PER_GROUP + e_idx).astype(jnp.int32)

    A = T * TOP_K
    B = MOE_ROWS
    n_blocks = -(-(A + N_EXPERTS * (B - 1)) // B)
    n_rows = n_blocks * B
    eid_f = eid.reshape(A)
    onehot = (eid_f[:, None] == jnp.arange(N_EXPERTS, dtype=jnp.int32)[None, :]).astype(jnp.int32)
    csum = jnp.cumsum(onehot, axis=0)
    rank = jnp.sum(onehot * csum, axis=1) - 1
    counts = csum[-1]
    padded = (counts + B - 1) // B * B
    pad_end = jnp.cumsum(padded)
    pad_start = pad_end - padded
    dest = jnp.sum(onehot * pad_start[None, :], axis=1) + rank
    row_a = jnp.full((n_rows,), -1, jnp.int32).at[dest].set(jnp.arange(A, dtype=jnp.int32))
    tok = jnp.maximum(row_a, 0) // TOP_K
    row_src = tok
    rows = jnp.arange(n_rows, dtype=jnp.int32)
    dump = TOP_K * T + ((rows // B) % 2) * B + rows % B
    row_dst = jnp.where(row_a < 0, dump, (row_a % TOP_K) * T + tok)
    blk_exp = jnp.minimum(
        jnp.sum((pad_end[None, :] <= (jnp.arange(n_blocks, dtype=jnp.int32) * B)[:, None]).astype(jnp.int32), axis=1),
        N_EXPERTS - 1).astype(jnp.int32)
    n_used = (pad_end[-1] // B).astype(jnp.int32).reshape(1)
    return blk_exp, row_src, row_dst, n_used, wts


def _moe_kernel(blk_ref, src_ref, dst_ref, nused_ref, hn_ref, w1_ref, w3_ref, w2_ref,
                y_ref, xg_sc, yo_sc, gsem, ssem, *, n_blocks):
    del blk_ref
    B = MOE_ROWS
    n_real = y_ref.shape[0] - 2 * B
    i = pl.program_id(0)
    n_used = nused_ref[0]
    slot = i % 2

    def start_gather(blk, s):
        def body(g, c):
            for u in range(ROW_DMA_UNROLL):
                r = g * ROW_DMA_UNROLL + u
                src = src_ref[blk * B + r]
                pltpu.make_async_copy(hn_ref.at[pl.ds(src, 1)], xg_sc.at[s, pl.ds(r, 1)],
                                      gsem.at[s]).start(priority=u % 2)
            return c
        lax.fori_loop(0, B // ROW_DMA_UNROLL, body, 0)

    def wait_gather(s):
        pltpu.make_async_copy(hn_ref.at[pl.ds(0, B)], xg_sc.at[s], gsem.at[s]).wait()

    def start_scatter(blk, s):
        def body(g, c):
            for u in range(ROW_DMA_UNROLL):
                r = g * ROW_DMA_UNROLL + u
                dst = dst_ref[blk * B + r]
                pltpu.make_async_copy(yo_sc.at[s, pl.ds(r, 1)], y_ref.at[pl.ds(dst, 1)],
                                      ssem.at[s]).start(priority=u % 2)
            return c
        lax.fori_loop(0, B // ROW_DMA_UNROLL, body, 0)

    def wait_scatter(s):
        pltpu.make_async_copy(yo_sc.at[s], y_ref.at[pl.ds(0, B)], ssem.at[s]).wait()

    @pl.when(i == 0)
    def _():
        start_gather(0, 0)
        yo_sc[0] = jnp.zeros(yo_sc.shape[1:], F32)
        for half in range(2):
            fill = pltpu.make_async_copy(yo_sc.at[0], y_ref.at[pl.ds(n_real + half * B, B)], ssem.at[0])
            fill.start()
            fill.wait()

    @pl.when(i + 1 < n_used)
    def _():
        start_gather(i + 1, 1 - slot)

    @pl.when(i < n_used)
    def _():
        wait_gather(slot)

        @pl.when(i >= 2)
        def _():
            wait_scatter(slot)

        xb = xg_sc[slot].astype(BF16)
        h1 = jnp.dot(xb, w1_ref[...].astype(BF16), preferred_element_type=F32)
        h3 = jnp.dot(xb, w3_ref[...].astype(BF16), preferred_element_type=F32)
        h = (h1 * jax.nn.sigmoid(h1) * h3).astype(BF16)
        yo_sc[slot] = jnp.dot(h, w2_ref[...].astype(BF16), preferred_element_type=F32)
        start_scatter(i, slot)

    @pl.when(i == n_blocks - 1)
    def _():
        wait_scatter((n_used - 1) % 2)

        @pl.when(n_used >= 2)
        def _():
            wait_scatter(n_used % 2)


def _moe(hn, blk_exp, row_src, row_dst, n_used, w1, w3, w2, *, layer):
    T, D = hn.shape
    B = MOE_ROWS
    n_blocks = blk_exp.shape[0]
    de = w1.shape[3]
    grid_spec = pltpu.PrefetchScalarGridSpec(
        num_scalar_prefetch=4,
        grid=(n_blocks,),
        in_specs=[
            pl.BlockSpec(memory_space=pl.ANY),
            pl.BlockSpec((None, None, D, de), lambda i, blk, src, dst, nu: (layer, blk[i], 0, 0)),
            pl.BlockSpec((None, None, D, de), lambda i, blk, src, dst, nu: (layer, blk[i], 0, 0)),
            pl.BlockSpec((None, None, de, D), lambda i, blk, src, dst, nu: (layer, blk[i], 0, 0)),
        ],
        out_specs=pl.BlockSpec(memory_space=pl.ANY),
        scratch_shapes=[
            pltpu.VMEM((2, B, D), F32),
            pltpu.VMEM((2, B, D), F32),
            pltpu.SemaphoreType.DMA((2,)),
            pltpu.SemaphoreType.DMA((2,)),
        ],
    )
    return pl.pallas_call(
        functools.partial(_moe_kernel, n_blocks=n_blocks),
        grid_spec=grid_spec,
        out_shape=jax.ShapeDtypeStruct((TOP_K * T + 2 * B, D), F32),
        compiler_params=_params(("arbitrary",)),
        name="moe",
    )(blk_exp, row_src, row_dst, n_used, hn, w1, w3, w2)


def _ple_kernel(x_ref, y0_ref, y1_ref, wt_ref, g_ref, p_ref, wg_ref, wp_ref, out_ref, x2_sc, hn_sc, *, tn):
    j = pl.program_id(1)

    @pl.when(j == 0)
    def _():
        wt = wt_ref[...]
        x2 = x_ref[...] + (wt[:, 0:1] * y0_ref[...] + wt[:, 1:2] * y1_ref[...])
        hn_sc[...] = _rms(x2, g_ref[...]).astype(BF16)
        for c in range(x2_sc.shape[0]):
            x2_sc[c] = x2[:, c * tn:(c + 1) * tn]

    gate = jax.nn.sigmoid(jnp.dot(hn_sc[...], wg_ref[...].astype(BF16), preferred_element_type=F32))
    proj = jnp.dot(p_ref[...].astype(BF16), wp_ref[...].astype(BF16), preferred_element_type=F32)
    out_ref[...] = x2_sc[j] + gate * proj


def _ple(x, y_slots, wts, g, p, wg, wp, *, layer):
    T, D = x.shape
    dp = p.shape[2]
    tm = _row_tile(T, 384)
    tn = 512
    nt = T // tm
    return pl.pallas_call(
        functools.partial(_ple_kernel, tn=tn),
        grid=(nt, D // tn),
        in_specs=[
            pl.BlockSpec((tm, D), lambda i, j: (i, 0)),
            pl.BlockSpec((tm, D), lambda i, j: (i, 0)),
            pl.BlockSpec((tm, D), lambda i, j: (i + nt, 0)),
            pl.BlockSpec((tm, TOP_K), lambda i, j: (i, 0)),
            _layer_vec(layer, D),
            pl.BlockSpec((None, tm, dp), lambda i, j: (layer, i, 0)),
            pl.BlockSpec((None, D, tn), lambda i, j: (layer, 0, j)),
            pl.BlockSpec((None, dp, tn), lambda i, j: (layer, 0, j)),
        ],
        out_specs=pl.BlockSpec((tm, tn), lambda i, j: (i, j)),
        out_shape=jax.ShapeDtypeStruct((T, D), F32),
        scratch_shapes=[pltpu.VMEM((D // tn, tm, tn), F32), pltpu.VMEM((tm, D), BF16)],
        compiler_params=_params(("parallel", "arbitrary")),
        name="ple",
    )(x, y_slots, y_slots, wts, g, p, wg, wp)


def kernel(x_prompt, x_sample, cache_k, cache_v, state_rnn_h, state_conv, p_prompt, p_sample, norm_mix, w_in, q_norm, k_norm, lambda_q1, lambda_k1, lambda_q2, lambda_k2, attn_out_norm, conv_w, conv_b, w_rg_a, b_rg_a, w_rg_x, b_rg_x, rg_lambda, rnn_out_norm, w_out, norm_ffn, w_router_group, b_router_group, w_router_expert, b_router_expert, w_exp_gate, w_exp_up, w_exp_down, norm_ple, w_ple_gate, w_ple_proj):
    bp, seq, D = x_prompt.shape
    nb, dseq, _ = x_sample.shape
    depth = w_in.shape[0]
    d_rnn = state_rnn_h.shape[-1]
    d_ple = p_prompt.shape[-1]
    assert bp == 1 and seq % CHUNK == 0 and dseq % 16 == 0 and seq % dseq == 0
    n_p = bp * seq
    n_s = nb * dseq
    T = n_p + n_s

    x = jnp.concatenate([x_prompt.reshape(n_p, D), x_sample.reshape(n_s, D)], axis=0)
    p_all = jnp.concatenate([p_prompt.reshape(depth, n_p, d_ple), p_sample.reshape(depth, n_s, d_ple)], axis=1)
    vec = lambda a: a.reshape(depth, 1, -1)
    conv_pad = jnp.pad(state_conv, ((0, 0), (0, 0), (SUBLANES - (CONV_W - 1), 0), (0, 0)))
    h0_s = state_rnn_h.reshape(depth, nb, 1, d_rnn)
    zero_h = jnp.zeros((1, bp, 1, d_rnn), F32)
    zero_conv = jnp.zeros((1, bp, SUBLANES, d_rnn), F32)
    pad_lanes = ROUTER_LANES - N_GROUPS - N_EXPERTS
    w_router = jnp.concatenate([w_router_group, w_router_expert, jnp.zeros((depth, D, pad_lanes), F32)], axis=-1)
    b_router = jnp.concatenate([b_router_group, b_router_expert, jnp.zeros((depth, pad_lanes), F32)], axis=-1)
    tb_p = _row_tile(seq, 256, SUBLANES)

    norm_mix_v, qg_v, kg_v = vec(norm_mix), vec(q_norm), vec(k_norm)
    lam_params = (vec(lambda_q1), vec(lambda_k1), vec(lambda_q2), vec(lambda_k2), vec(attn_out_norm))
    rnn_w = (conv_w, vec(conv_b), w_rg_a, vec(b_rg_a), w_rg_x, vec(b_rg_x), vec(rg_lambda), vec(rnn_out_norm))
    norm_ffn_v, b_router_v, norm_ple_v = vec(norm_ffn), vec(b_router), vec(norm_ple)

    ks, vs = [], []
    hp, cp, hs_, cs_ = [], [], [], []
    for l in range(depth):
        lam0 = _lambda_init(l)
        q, k, v, xr, gate, kb, vb = _in_proj(x, norm_mix_v, w_in, qg_v, kg_v, layer=l)
        ks.append(k)
        vs.append(v)
        o = jnp.zeros((T, D_ATTN), BF16)
        o = _attn_prompt(q, kb, vb, lam_params, qg_v, kg_v, o, layer=l, n_prompt=n_p, lam0=lam0)
        o = _attn_sample(q, kb, vb, cache_k, cache_v, lam_params, o, layer=l, n_prompt=n_p, rows=dseq, lam0=lam0)

        y = jnp.zeros((T, d_rnn), BF16)
        y, h_p, c_p = _rnn(xr, gate, zero_h, zero_conv, rnn_w, y, layer=l, state_layer=0,
                           row0=0, n_seq=bp, seq_len=seq, tb=tb_p)
        y, h_s, c_s = _rnn(xr, gate, h0_s, conv_pad, rnn_w, y, layer=l, state_layer=l,
                           row0=n_p, n_seq=nb, seq_len=dseq, tb=dseq)

        x = _out_proj(o, y, w_out, x, layer=l)

        hn, logits = _router(x, norm_ffn_v, w_router, b_router_v, layer=l)
        blk_exp, row_src, row_dst, n_used, wts = _routing_tables(logits, T)
        y_slots = _moe(hn, blk_exp, row_src, row_dst, n_used, w_exp_gate, w_exp_up, w_exp_down, layer=l)

        x = _ple(x, y_slots, wts, norm_ple_v, p_all, w_ple_gate, w_ple_proj, layer=l)

        hp.append(h_p[:, 0])
        cp.append(c_p[:, SUBLANES - (CONV_W - 1):])
        hs_.append(h_s[:, 0])
        cs_.append(c_s[:, SUBLANES - (CONV_W - 1):])

    k_p, k_s, v_p, v_s = _kv_layout(ks, vs, n_prompt=n_p, n_sample=n_s)
    return (x[:n_p].reshape(bp, seq, D), x[n_p:].reshape(nb, dseq, D),
            k_p.reshape(depth, bp, seq, N_HEADS, QK_DIM), v_p.reshape(depth, bp, seq, N_HEADS, V_DIM),
            jnp.stack(hp), jnp.stack(cp),
            k_s.reshape(depth, nb, dseq, N_HEADS, QK_DIM), v_s.reshape(depth, nb, dseq, N_HEADS, V_DIM),
            jnp.stack(hs_), jnp.stack(cs_))
```

```python
import functools
import math

import jax
import jax.numpy as jnp
from jax import lax
from jax.experimental import pallas as pl
from jax.experimental.pallas import tpu as pltpu

F32 = jnp.float32
BF16 = jnp.bfloat16

CHUNK = 64
N_HEADS = 8
HEAD_DIM = 64
QK_DIM = 2 * HEAD_DIM
V_DIM = 2 * HEAD_DIM
D_ATTN = N_HEADS * V_DIM
N_RNN_BLOCKS = 8
CONV_W = 4
RG_C = 8.0
N_GROUPS = 4
EXPERTS_PER_GROUP = 8
N_EXPERTS = N_GROUPS * EXPERTS_PER_GROUP
TOP_K = 2
EPS = 1e-6
NEG_INF = -1e30

LANES = 128
SUBLANES = 8
VMEM_LIMIT_BYTES = 56 * 1024 * 1024
MOE_ROWS = 256
ROUTER_LANES = 128
ROW_DMA_UNROLL = 8


def _lambda_init(layer):
    return 0.8 - 0.6 * math.exp(-0.3 * layer)


def _row_tile(total, target, multiple=16):
    best = None
    for t in range(multiple, min(total, target) + 1, multiple):
        if total % t == 0:
            best = t
    assert best is not None, (total, target)
    return best


def _params(sem):
    return pltpu.CompilerParams(dimension_semantics=sem, vmem_limit_bytes=VMEM_LIMIT_BYTES)


def _layer_vec(l, n):
    return pl.BlockSpec((None, 1, n), lambda *_: (l, 0, 0))


def _rms(x, gain):
    ms = jnp.mean(x * x, axis=-1, keepdims=True)
    return x * lax.rsqrt(ms + EPS) * gain


def _half_norm(a, gain, lo):
    sq = a * a
    s_lo = jnp.sum(jnp.where(lo, sq, 0.0), axis=-1, keepdims=True)
    s_hi = jnp.sum(jnp.where(lo, 0.0, sq), axis=-1, keepdims=True)
    ms = jnp.where(lo, s_lo, s_hi) * (1.0 / HEAD_DIM)
    return a * lax.rsqrt(ms + EPS) * gain


def _in_proj_kernel(x_ref, g_ref, w_ref, qg_ref, kg_ref,
                    q_ref, k_ref, v_ref, xr_ref, gate_ref, kb_ref, vb_ref, hn_ref, *, steps_per_section, tn):
    j = pl.program_id(1)

    @pl.when(j == 0)
    def _():
        hn_ref[...] = _rms(x_ref[...], g_ref[...]).astype(BF16)

    acc = jnp.dot(hn_ref[...], w_ref[...].astype(BF16), preferred_element_type=F32)
    sec = j // steps_per_section
    lo = lax.broadcasted_iota(jnp.int32, (1, LANES), 1) < HEAD_DIM

    @pl.when(sec == 0)
    def _():
        for h in range(tn // LANES):
            sl = slice(h * LANES, (h + 1) * LANES)
            q_ref[:, sl] = (_half_norm(acc[:, sl], qg_ref[...], lo) * (HEAD_DIM ** -0.5)).astype(BF16)

    @pl.when(sec == 1)
    def _():
        for h in range(tn // LANES):
            sl = slice(h * LANES, (h + 1) * LANES)
            kn = _half_norm(acc[:, sl], kg_ref[...], lo)
            k_ref[:, sl] = kn
            kb_ref[:, sl] = kn.astype(BF16)

    @pl.when(sec == 2)
    def _():
        v_ref[...] = acc
        vb_ref[...] = acc.astype(BF16)

    @pl.when(sec == 3)
    def _():
        xr_ref[...] = acc

    @pl.when(sec == 4)
    def _():
        gate_ref[...] = acc


def _in_proj(x, g, w_in, qg, kg, *, layer):
    T, D = x.shape
    tm = _row_tile(T, 1056)
    tn = 512
    sec_w = D_ATTN
    sps = sec_w // tn
    n_steps = w_in.shape[2] // tn
    col = lambda sec: (lambda j: jnp.clip(j - sec * sps, 0, sps - 1))

    def out_spec(sec):
        return pl.BlockSpec((tm, tn), lambda i, j: (i, col(sec)(j)))

    flat = jax.ShapeDtypeStruct((T, sec_w), F32)
    flat_b = jax.ShapeDtypeStruct((T, sec_w), BF16)
    return pl.pallas_call(
        functools.partial(_in_proj_kernel, steps_per_section=sps, tn=tn),
        grid=(T // tm, n_steps),
        in_specs=[
            pl.BlockSpec((tm, D), lambda i, j: (i, 0), pipeline_mode=pl.Buffered(1)),
            _layer_vec(layer, D),
            pl.BlockSpec((None, D, tn), lambda i, j: (layer, 0, j)),
            _layer_vec(layer, LANES),
            _layer_vec(layer, LANES),
        ],
        out_specs=[out_spec(s) for s in (0, 1, 2, 3, 4, 1, 2)],
        out_shape=[flat_b, flat, flat, flat, flat, flat_b, flat_b],
        scratch_shapes=[pltpu.VMEM((tm, D), BF16)],
        compiler_params=_params(("parallel", "arbitrary")),
        name="in_proj",
    )(x, g, w_in, qg, kg)


def _kv_layout_kernel(*refs, depth, prompt_tiles):
    srcs = refs[:2 * depth]
    kp_ref, ks_ref, vp_ref, vs_ref = refs[2 * depth:]
    l = pl.program_id(0)
    i = pl.program_id(1)

    def spread(src, dst):
        for h in range(N_HEADS):
            dst[:, h, :] = src[:, h * LANES:(h + 1) * LANES]

    for d in range(depth):
        @pl.when(jnp.logical_and(l == d, i < prompt_tiles))
        def _():
            spread(srcs[d], kp_ref)
            spread(srcs[depth + d], vp_ref)

        @pl.when(jnp.logical_and(l == d, i >= prompt_tiles))
        def _():
            spread(srcs[d], ks_ref)
            spread(srcs[depth + d], vs_ref)


def _kv_layout(ks, vs, *, n_prompt, n_sample):
    depth = len(ks)
    T, width = ks[0].shape
    tr = math.gcd(n_prompt, n_sample)
    nt, np_t, ns_t = T // tr, n_prompt // tr, n_sample // tr

    def src_spec(d):
        return pl.BlockSpec((tr, width), lambda l, i: (jnp.where(l < d, 0, jnp.where(l > d, nt - 1, i)), 0))

    blk = (None, tr, N_HEADS, LANES)
    prompt_spec = pl.BlockSpec(blk, lambda l, i: (l, jnp.minimum(i, np_t - 1), 0, 0))
    sample_spec = pl.BlockSpec(blk, lambda l, i: (l, jnp.clip(i - np_t, 0, ns_t - 1), 0, 0))
    shape = lambda n: jax.ShapeDtypeStruct((depth, n, N_HEADS, LANES), F32)
    return pl.pallas_call(
        functools.partial(_kv_layout_kernel, depth=depth, prompt_tiles=np_t),
        grid=(depth, nt),
        in_specs=[src_spec(d) for d in range(depth)] * 2,
        out_specs=[prompt_spec, sample_spec, prompt_spec, sample_spec],
        out_shape=[shape(n_prompt), shape(n_sample), shape(n_prompt), shape(n_sample)],
        compiler_params=_params(("arbitrary", "arbitrary")),
        name="kv_layout",
    )(*ks, *vs)


def _lambda_value(lq1_ref, lk1_ref, lq2_ref, lk2_ref, lam0):
    s1 = jnp.sum(lq1_ref[...] * lk1_ref[...], axis=-1, keepdims=True)
    s2 = jnp.sum(lq2_ref[...] * lk2_ref[...], axis=-1, keepdims=True)
    return jnp.exp(s1) - jnp.exp(s2) + lam0


def _split_q(q):
    lo = lax.broadcasted_iota(jnp.int32, q.shape, 1) < HEAD_DIM
    zero = jnp.zeros_like(q)
    return jnp.concatenate([jnp.where(lo, q, zero), jnp.where(lo, zero, q)], axis=0)


def _scores(q2, kb):
    return lax.dot_general(q2, kb, (((1,), (1,)), ((), ())), preferred_element_type=F32)


def _finish_heads(acc, l, lam, gout, rows, lam0):
    o = acc[:rows] / l[:rows] - lam * (acc[rows:] / l[rows:])
    return _rms(o, gout) * (1.0 - lam0)


def _attn_prompt_kernel(q_ref, k_ref, v_ref, lq1_ref, lk1_ref, lq2_ref, lk2_ref, gout_ref, o_init_ref,
                        o_ref, q2_sc, m_sc, acc_sc, *, bq, lam0):
    del o_init_ref
    i = pl.program_id(1)
    bk = bq
    q2_sc[...] = _split_q(q_ref[...])
    m_sc[...] = jnp.full(m_sc.shape, NEG_INF, F32)
    acc_sc[...] = jnp.zeros(acc_sc.shape, F32)
    ones = jnp.ones((bk, LANES), BF16)

    def update(off, masked):
        kb = k_ref[pl.ds(off, bk), :]
        vb = v_ref[pl.ds(off, bk), :]
        s = _scores(q2_sc[...], kb)
        if masked:
            q_chunk = (lax.broadcasted_iota(jnp.int32, s.shape, 0) % bq) // CHUNK
            k_chunk = lax.broadcasted_iota(jnp.int32, s.shape, 1) // CHUNK
            s = jnp.where(k_chunk <= q_chunk, s, NEG_INF)
        m_old = m_sc[...]
        m_new = jnp.maximum(m_old, jnp.max(s, axis=-1, keepdims=True))
        pr = jnp.exp(s - jnp.concatenate([m_new] * (bk // LANES), axis=1))
        alpha = jnp.exp(m_old - m_new)
        pv = jnp.dot(pr.astype(BF16), jnp.concatenate([vb, ones], axis=1), preferred_element_type=F32)
        acc_sc[...] = jnp.concatenate([alpha, alpha], axis=1) * acc_sc[...] + pv
        m_sc[...] = m_new

    def body(j, carry):
        update(pl.multiple_of(j * bk, bk), False)
        return carry

    lax.fori_loop(0, i, body, 0)
    update(pl.multiple_of(i * bk, bk), True)
    lam = _lambda_value(lq1_ref, lk1_ref, lq2_ref, lk2_ref, lam0)
    acc = acc_sc[...]
    o_ref[...] = _finish_heads(acc[:, :LANES], acc[:, LANES:], lam, gout_ref[...], bq, lam0).astype(o_ref.dtype)


def _score_bound(qg_ref, kg_ref):
    gq = jnp.max(jnp.abs(qg_ref[...]), axis=-1, keepdims=True)
    gk = jnp.max(jnp.abs(kg_ref[...]), axis=-1, keepdims=True)
    return (HEAD_DIM ** 0.5) * gq * gk


def _attn_prompt_shift_kernel(q_ref, k_ref, v_ref, lq1_ref, lk1_ref, lq2_ref, lk2_ref, gout_ref, qg_ref, kg_ref,
                              o_init_ref, o_ref, q2_sc, acc_sc, *, bq, lam0):
    del o_init_ref
    i = pl.program_id(1)
    bk = bq
    q2_sc[...] = _split_q(q_ref[...])
    acc_sc[...] = jnp.zeros(acc_sc.shape, F32)
    ones = jnp.ones((bk, LANES), BF16)
    shift = _score_bound(qg_ref, kg_ref)

    def update(off, masked):
        kb = k_ref[pl.ds(off, bk), :]
        vb = v_ref[pl.ds(off, bk), :]
        s = _scores(q2_sc[...], kb) - shift
        if masked:
            q_chunk = (lax.broadcasted_iota(jnp.int32, s.shape, 0) % bq) // CHUNK
            k_chunk = lax.broadcasted_iota(jnp.int32, s.shape, 1) // CHUNK
            s = jnp.where(k_chunk <= q_chunk, s, NEG_INF)
        pr = jnp.exp(s).astype(BF16)
        acc_sc[...] += jnp.dot(pr, jnp.concatenate([vb, ones], axis=1), preferred_element_type=F32)

    def body(j, carry):
        update(pl.multiple_of(j * bk, bk), False)
        return carry

    lax.fori_loop(0, i, body, 0)
    update(pl.multiple_of(i * bk, bk), True)
    lam = _lambda_value(lq1_ref, lk1_ref, lq2_ref, lk2_ref, lam0)
    acc = acc_sc[...]
    o_ref[...] = _finish_heads(acc[:, :LANES], acc[:, LANES:], lam, gout_ref[...], bq, lam0).astype(o_ref.dtype)


MAX_FIXED_SHIFT = 30.0


def _attn_prompt(q, kb, vb, lam_params, qg, kg, o_init, *, layer, n_prompt, lam0):
    bq = _row_tile(n_prompt, 1024, CHUNK)
    common_in = [
        pl.BlockSpec((bq, LANES), lambda h, i: (i, h)),
        pl.BlockSpec((n_prompt, LANES), lambda h, i: (0, h)),
        pl.BlockSpec((n_prompt, LANES), lambda h, i: (0, h)),
        _layer_vec(layer, HEAD_DIM), _layer_vec(layer, HEAD_DIM), _layer_vec(layer, HEAD_DIM),
        _layer_vec(layer, HEAD_DIM), _layer_vec(layer, V_DIM),
    ]
    common = dict(
        grid=(N_HEADS, n_prompt // bq),
        out_specs=pl.BlockSpec((bq, LANES), lambda h, i: (i, h)),
        out_shape=jax.ShapeDtypeStruct(o_init.shape, o_init.dtype),
        compiler_params=_params(("parallel", "arbitrary")),
    )
    q2_scratch = pltpu.VMEM((2 * bq, LANES), BF16)
    acc_scratch = pltpu.VMEM((2 * bq, 2 * LANES), F32)

    def streaming(o):
        return pl.pallas_call(
            functools.partial(_attn_prompt_kernel, bq=bq, lam0=lam0),
            in_specs=common_in + [pl.BlockSpec(memory_space=pl.ANY)],
            scratch_shapes=[q2_scratch, pltpu.VMEM((2 * bq, LANES), F32), acc_scratch],
            input_output_aliases={8: 0},
            name="attn_prompt",
            **common,
        )(q, kb, vb, *lam_params, o)

    def fixed_shift(o):
        return pl.pallas_call(
            functools.partial(_attn_prompt_shift_kernel, bq=bq, lam0=lam0),
            in_specs=common_in + [_layer_vec(layer, LANES), _layer_vec(layer, LANES),
                                  pl.BlockSpec(memory_space=pl.ANY)],
            scratch_shapes=[q2_scratch, acc_scratch],
            input_output_aliases={10: 0},
            name="attn_prompt_shift",
            **common,
        )(q, kb, vb, *lam_params, qg, kg, o)

    bound = (HEAD_DIM ** 0.5) * jnp.max(jnp.abs(qg[layer])) * jnp.max(jnp.abs(kg[layer]))
    return lax.cond(bound <= MAX_FIXED_SHIFT, fixed_shift, streaming, o_init)


def _attn_sample_kernel(q_ref, k_ref, v_ref, ck_ref, cv_ref, lq1_ref, lk1_ref, lq2_ref, lk2_ref, gout_ref,
                        o_init_ref, o_ref, *, rows, lam0):
    del o_init_ref
    lam = _lambda_value(lq1_ref, lk1_ref, lq2_ref, lk2_ref, lam0)
    for h in range(N_HEADS):
        sl = slice(h * LANES, (h + 1) * LANES)
        q2 = _split_q(q_ref[:, sl])
        s_c = _scores(q2, ck_ref[:, h, :].astype(BF16))
        s_n = _scores(q2, k_ref[:, sl])
        m = jnp.maximum(jnp.max(s_c, axis=-1, keepdims=True), jnp.max(s_n, axis=-1, keepdims=True))
        p_c = jnp.exp(s_c - m)
        p_n = jnp.exp(s_n - m)
        l = jnp.sum(p_c, axis=-1, keepdims=True) + jnp.sum(p_n, axis=-1, keepdims=True)
        acc = (jnp.dot(p_c.astype(BF16), cv_ref[:, h, :].astype(BF16), preferred_element_type=F32)
               + jnp.dot(p_n.astype(BF16), v_ref[:, sl], preferred_element_type=F32))
        o_ref[:, sl] = _finish_heads(acc, l, lam, gout_ref[...], rows, lam0).astype(o_ref.dtype)


def _attn_sample(q, kb, vb, cache_k, cache_v, lam_params, o_init, *, layer, n_prompt, rows, lam0):
    n_streams, past = cache_k.shape[1], cache_k.shape[2]
    base = n_prompt // rows
    row_spec = pl.BlockSpec((rows, D_ATTN), lambda b: (base + b, 0))
    cache_spec = pl.BlockSpec((None, None, past, N_HEADS, LANES), lambda b: (layer, b, 0, 0, 0))
    return pl.pallas_call(
        functools.partial(_attn_sample_kernel, rows=rows, lam0=lam0),
        grid=(n_streams,),
        in_specs=[row_spec, row_spec, row_spec, cache_spec, cache_spec,
                  _layer_vec(layer, HEAD_DIM), _layer_vec(layer, HEAD_DIM), _layer_vec(layer, HEAD_DIM),
                  _layer_vec(layer, HEAD_DIM), _layer_vec(layer, V_DIM),
                  pl.BlockSpec(memory_space=pl.ANY)],
        out_specs=row_spec,
        out_shape=jax.ShapeDtypeStruct(o_init.shape, o_init.dtype),
        input_output_aliases={10: 0},
        compiler_params=_params(("parallel",)),
        name="attn_sample",
    )(q, kb, vb, cache_k, cache_v, *lam_params, o_init)


def _gelu_tanh(x):
    return 0.5 * x * (1.0 + jnp.tanh(math.sqrt(2.0 / math.pi) * (x + 0.044715 * (x * x * x))))


def _rnn_kernel(xr_ref, gate_ref, h0_ref, cbuf_ref, cw_ref, cb_ref, wa_ref, ba_ref, wx_ref, bx_ref,
                lam_ref, gn_ref, y_init_ref, y_ref, hlast_ref, ctail_ref, h_sc, tail_sc, xc_sc, *, tb):
    del y_init_ref
    t = pl.program_id(1)

    @pl.when(t == 0)
    def _():
        h_sc[...] = h0_ref[...]
        tail_sc[...] = cbuf_ref[...]

    x = xr_ref[...]
    cw = cw_ref[...]
    cb = cb_ref[...]

    def conv(window, rows):
        acc = cb + window * cw[CONV_W - 1:CONV_W]
        for back in range(1, CONV_W):
            acc = acc + pltpu.roll(window, back, axis=0) * cw[CONV_W - 1 - back:CONV_W - back]
        return acc[-rows:]

    xc_sc[...] = conv(x, tb)
    head = jnp.concatenate([tail_sc[...], x[:SUBLANES]], axis=0)
    xc_sc[0:SUBLANES, :] = conv(head, SUBLANES)
    tail_sc[...] = x[tb - SUBLANES:]
    ctail_ref[...] = x[tb - SUBLANES:]
    xc = xc_sc[...]

    xb = xc.astype(BF16)
    r_parts, i_parts = [], []
    for n in range(N_RNN_BLOCKS):
        sl = slice(n * LANES, (n + 1) * LANES)
        r_parts.append(jnp.dot(xb[:, sl], wa_ref[n].astype(BF16), preferred_element_type=F32))
        i_parts.append(jnp.dot(xb[:, sl], wx_ref[n].astype(BF16), preferred_element_type=F32))
    r = jax.nn.sigmoid(jnp.concatenate(r_parts, axis=-1) + ba_ref[...])
    ig = jax.nn.sigmoid(jnp.concatenate(i_parts, axis=-1) + bx_ref[...])
    neg_lam = -lam_ref[...]
    softplus = jnp.maximum(neg_lam, 0.0) + jnp.log1p(jnp.exp(-jnp.abs(neg_lam)))
    log_a = (-RG_C * r) * softplus
    a = jnp.exp(log_a)
    u = jnp.sqrt(1.0 - a * a) * (ig * xc)

    row = lax.broadcasted_iota(jnp.int32, (tb, 1), 0)
    d = 1
    while d < tb:
        keep = row >= d
        u = jnp.where(keep, a * pltpu.roll(u, d, axis=0) + u, u)
        a = jnp.where(keep, a * pltpu.roll(a, d, axis=0), a)
        d *= 2
    hs = u + a * h_sc[...]
    h_sc[...] = hs[tb - 1:tb]
    hlast_ref[...] = hs[tb - 1:tb]

    y_ref[...] = _rms(hs * _gelu_tanh(gate_ref[...]), gn_ref[...]).astype(y_ref.dtype)


def _rnn(xr, gate, h0, cbuf, weights, y_init, *, layer, state_layer, row0, n_seq, seq_len, tb):
    C = xr.shape[1]
    nb = seq_len // tb
    base = row0 // tb
    row_spec = pl.BlockSpec((tb, C), lambda b, t: (base + b * nb + t, 0))
    blk = lambda: pl.BlockSpec((None, N_RNN_BLOCKS, LANES, LANES), lambda b, t: (layer, 0, 0, 0))
    return pl.pallas_call(
        functools.partial(_rnn_kernel, tb=tb),
        grid=(n_seq, nb),
        in_specs=[row_spec, row_spec,
                  pl.BlockSpec((None, None, 1, C), lambda b, t: (state_layer, b, 0, 0)),
                  pl.BlockSpec((None, None, SUBLANES, C), lambda b, t: (state_layer, b, 0, 0)),
                  pl.BlockSpec((None, CONV_W, C), lambda b, t: (layer, 0, 0)), _layer_vec(layer, C),
                  blk(), _layer_vec(layer, C), blk(), _layer_vec(layer, C), _layer_vec(layer, C),
                  _layer_vec(layer, C),
                  pl.BlockSpec(memory_space=pl.ANY)],
        out_specs=[row_spec,
                   pl.BlockSpec((None, 1, C), lambda b, t: (b, 0, 0)),
                   pl.BlockSpec((None, SUBLANES, C), lambda b, t: (b, 0, 0))],
        out_shape=[jax.ShapeDtypeStruct(y_init.shape, y_init.dtype),
                   jax.ShapeDtypeStruct((n_seq, 1, C), F32),
                   jax.ShapeDtypeStruct((n_seq, SUBLANES, C), F32)],
        scratch_shapes=[pltpu.VMEM((1, C), F32), pltpu.VMEM((SUBLANES, C), F32), pltpu.VMEM((tb, C), F32)],
        input_output_aliases={12: 0},
        compiler_params=_params(("arbitrary", "arbitrary")),
        name="rnn",
    )(xr, gate, h0, cbuf, *weights, y_init)


def _out_proj_kernel(o_ref, y_ref, wa_ref, wb_ref, x_ref, out_ref):
    acc = jnp.dot(o_ref[...], wa_ref[...].astype(BF16), preferred_element_type=F32)
    acc = acc + jnp.dot(y_ref[...], wb_ref[...].astype(BF16), preferred_element_type=F32)
    out_ref[...] = x_ref[...] + acc


def _out_proj(o, y, w_out, x, *, layer):
    T, D = x.shape
    half = o.shape[1]
    tm = _row_tile(T, 768)
    tn = 512
    return pl.pallas_call(
        _out_proj_kernel,
        grid=(T // tm, D // tn),
        in_specs=[
            pl.BlockSpec((tm, half), lambda i, j: (i, 0)),
            pl.BlockSpec((tm, half), lambda i, j: (i, 0)),
            pl.BlockSpec((None, half, tn), lambda i, j: (layer, 0, j)),
            pl.BlockSpec((None, half, tn), lambda i, j: (layer, 1, j)),
            pl.BlockSpec((tm, tn), lambda i, j: (i, j)),
        ],
        out_specs=pl.BlockSpec((tm, tn), lambda i, j: (i, j)),
        out_shape=jax.ShapeDtypeStruct((T, D), F32),
        compiler_params=_params(("parallel", "arbitrary")),
        name="out_proj",
    )(o, y, w_out, w_out, x)


def _router_kernel(x_ref, g_ref, w_ref, b_ref, hn_ref, logit_ref):
    hn = _rms(x_ref[...], g_ref[...])
    hn_ref[...] = hn
    logit_ref[...] = jnp.dot(hn.astype(BF16), w_ref[...].astype(BF16), preferred_element_type=F32) + b_ref[...]


def _router(x, g, w, b, *, layer):
    T, D = x.shape
    tm = _row_tile(T, 768)
    return pl.pallas_call(
        _router_kernel,
        grid=(T // tm,),
        in_specs=[pl.BlockSpec((tm, D), lambda i: (i, 0)),
                  _layer_vec(layer, D),
                  pl.BlockSpec((None, D, ROUTER_LANES), lambda i: (layer, 0, 0)),
                  _layer_vec(layer, ROUTER_LANES)],
        out_specs=[pl.BlockSpec((tm, D), lambda i: (i, 0)),
                   pl.BlockSpec((tm, ROUTER_LANES), lambda i: (i, 0))],
        out_shape=[jax.ShapeDtypeStruct((T, D), F32), jax.ShapeDtypeStruct((T, ROUTER_LANES), F32)],
        compiler_params=_params(("parallel",)),
        name="router",
    )(x, g, w, b)


def _top1(x):
    val = jnp.max(x, axis=-1, keepdims=True)
    n = x.shape[-1]
    idx = jnp.min(jnp.where(x == val, lax.broadcasted_iota(jnp.int32, x.shape, x.ndim - 1), n), axis=-1,
                  keepdims=True)
    return val, idx


def _routing_tables(logits, n_tokens):
    T = n_tokens
    gp = jax.nn.softmax(logits[:, :N_GROUPS], axis=-1)
    g_val, g_idx = _top1(gp)
    el = logits[:, N_GROUPS:N_GROUPS + N_EXPERTS].reshape(T, N_GROUPS, EXPERTS_PER_GROUP)
    group = lax.broadcasted_iota(jnp.int32, el.shape, 1)
    el_g = jnp.sum(jnp.where(group == g_idx[:, :, None], el, 0.0), axis=1)
    v0, i0 = _top1(el_g)
    lane = lax.broadcasted_iota(jnp.int32, el_g.shape, 1)
    v1, i1 = _top1(jnp.where(lane == i0, -jnp.inf, el_g))
    e_val = jnp.concatenate([v0, v1], axis=-1)
    e_idx = jnp.concatenate([i0, i1], axis=-1)
    wts = g_val * jax.nn.softmax(e_val, axis=-1)
    eid = (g_idx * EXPERTS_PER_GROUP + e_idx).astype(jnp.int32)

    A = T * TOP_K
    B = MOE_ROWS
    n_blocks = -(-(A + N_EXPERTS * (B - 1)) // B)
    n_rows = n_blocks * B
    eid_f = eid.reshape(A)
    onehot = (eid_f[:, None] == jnp.arange(N_EXPERTS, dtype=jnp.int32)[None, :]).astype(jnp.int32)
    csum = jnp.cumsum(onehot, axis=0)
    rank = jnp.sum(onehot * csum, axis=1) - 1
    counts = csum[-1]
    padded = (counts + B - 1) // B * B
    pad_end = jnp.cumsum(padded)
    pad_start = pad_end - padded
    dest = jnp.sum(onehot * pad_start[None, :], axis=1) + rank
    row_a = jnp.full((n_rows,), -1, jnp.int32).at[dest].set(jnp.arange(A, dtype=jnp.int32))
    tok = jnp.maximum(row_a, 0) // TOP_K
    row_src = tok
    rows = jnp.arange(n_rows, dtype=jnp.int32)
    dump = TOP_K * T + ((rows // B) % 2) * B + rows % B
    row_dst = jnp.where(row_a < 0, dump, (row_a % TOP_K) * T + tok)
    blk_exp = jnp.minimum(
        jnp.sum((pad_end[None, :] <= (jnp.arange(n_blocks, dtype=jnp.int32) * B)[:, None]).astype(jnp.int32), axis=1),
        N_EXPERTS - 1).astype(jnp.int32)
    n_used = (pad_end[-1] // B).astype(jnp.int32).reshape(1)
    return blk_exp, row_src, row_dst, n_used, wts


def _moe_kernel(blk_ref, src_ref, dst_ref, nused_ref, hn_ref, w1_ref, w3_ref, w2_ref,
                y_ref, xg_sc, yo_sc, gsem, ssem, *, n_blocks):
    del blk_ref
    B = MOE_ROWS
    n_real = y_ref.shape[0] - 2 * B
    i = pl.program_id(0)
    n_used = nused_ref[0]
    slot = i % 2

    def start_gather(blk, s):
        def body(g, c):
            for u in range(ROW_DMA_UNROLL):
                r = g * ROW_DMA_UNROLL + u
                src = src_ref[blk * B + r]
                pltpu.make_async_copy(hn_ref.at[pl.ds(src, 1)], xg_sc.at[s, pl.ds(r, 1)],
                                      gsem.at[s]).start(priority=u % 2)
            return c
        lax.fori_loop(0, B // ROW_DMA_UNROLL, body, 0)

    def wait_gather(s):
        pltpu.make_async_copy(hn_ref.at[pl.ds(0, B)], xg_sc.at[s], gsem.at[s]).wait()

    def start_scatter(blk, s):
        def body(g, c):
            for u in range(ROW_DMA_UNROLL):
                r = g * ROW_DMA_UNROLL + u
                dst = dst_ref[blk * B + r]
                pltpu.make_async_copy(yo_sc.at[s, pl.ds(r, 1)], y_ref.at[pl.ds(dst, 1)],
                                      ssem.at[s]).start(priority=u % 2)
            return c
        lax.fori_loop(0, B // ROW_DMA_UNROLL, body, 0)

    def wait_scatter(s):
        pltpu.make_async_copy(yo_sc.at[s], y_ref.at[pl.ds(0, B)], ssem.at[s]).wait()

    @pl.when(i == 0)
    def _():
        start_gather(0, 0)
        yo_sc[...] = jnp.zeros(yo_sc.shape, F32)
        for half in range(2):
            fill = pltpu.make_async_copy(yo_sc.at[0], y_ref.at[pl.ds(n_real + half * B, B)], ssem.at[0])
            fill.start()
            fill.wait()

    nxt = jnp.minimum(i + 1, n_blocks - 1)
    prev = jnp.maximum(i - 1, 0)

    def active(s):
        wait_gather(s)

        @pl.when(i >= 1)
        def _():
            wait_scatter(s)

        xb = xg_sc[s].astype(BF16)
        for r in range(B):
            src = src_ref[nxt * B + r]
            pltpu.make_async_copy(hn_ref.at[pl.ds(src, 1)], xg_sc.at[1 - s, pl.ds(r, 1)],
                                  gsem.at[1 - s]).start(priority=r % 2)
            dst = jnp.where(i > 0, dst_ref[prev * B + r], n_real + B + r)
            pltpu.make_async_copy(yo_sc.at[1 - s, pl.ds(r, 1)], y_ref.at[pl.ds(dst, 1)],
                                  ssem.at[1 - s]).start(priority=(r + 1) % 2)
        h1 = jnp.dot(xb, w1_ref[...].astype(BF16), preferred_element_type=F32)
        h3 = jnp.dot(xb, w3_ref[...].astype(BF16), preferred_element_type=F32)
        h = (h1 * jax.nn.sigmoid(h1) * h3).astype(BF16)
        yo_sc[s] = jnp.dot(h, w2_ref[...].astype(BF16), preferred_element_type=F32)

    for s in range(2):
        pl.when(jnp.logical_and(i < n_used, slot == s))(functools.partial(active, s))

    @pl.when(i == n_blocks - 1)
    def _():
        last = n_used - 1
        start_scatter(last, last % 2)
        wait_scatter(last % 2)
        wait_scatter(1 - last % 2)
        wait_gather(n_used % 2)


def _moe(hn, blk_exp, row_src, row_dst, n_used, w1, w3, w2, *, layer):
    T, D = hn.shape
    B = MOE_ROWS
    n_blocks = blk_exp.shape[0]
    de = w1.shape[3]
    grid_spec = pltpu.PrefetchScalarGridSpec(
        num_scalar_prefetch=4,
        grid=(n_blocks,),
        in_specs=[
            pl.BlockSpec(memory_space=pl.ANY),
            pl.BlockSpec((None, None, D, de), lambda i, blk, src, dst, nu: (layer, blk[i], 0, 0)),
            pl.BlockSpec((None, None, D, de), lambda i, blk, src, dst, nu: (layer, blk[i], 0, 0)),
            pl.BlockSpec((None, None, de, D), lambda i, blk, src, dst, nu: (layer, blk[i], 0, 0)),
        ],
        out_specs=pl.BlockSpec(memory_space=pl.ANY),
        scratch_shapes=[
            pltpu.VMEM((2, B, D), F32),
            pltpu.VMEM((2, B, D), F32),
            pltpu.SemaphoreType.DMA((2,)),
            pltpu.SemaphoreType.DMA((2,)),
        ],
    )
    return pl.pallas_call(
        functools.partial(_moe_kernel, n_blocks=n_blocks),
        grid_spec=grid_spec,
        out_shape=jax.ShapeDtypeStruct((TOP_K * T + 2 * B, D), F32),
        compiler_params=_params(("arbitrary",)),
        name="moe",
    )(blk_exp, row_src, row_dst, n_used, hn, w1, w3, w2)


def _ple_kernel(x_ref, y0_ref, y1_ref, wt_ref, g_ref, p_ref, wg_ref, wp_ref, out_ref, x2_sc, hn_sc, *, tn):
    j = pl.program_id(1)

    @pl.when(j == 0)
    def _():
        wt = wt_ref[...]
        x2 = x_ref[...] + (wt[:, 0:1] * y0_ref[...] + wt[:, 1:2] * y1_ref[...])
        hn_sc[...] = _rms(x2, g_ref[...]).astype(BF16)
        for c in range(x2_sc.shape[0]):
            x2_sc[c] = x2[:, c * tn:(c + 1) * tn]

    gate = jax.nn.sigmoid(jnp.dot(hn_sc[...], wg_ref[...].astype(BF16), preferred_element_type=F32))
    proj = jnp.dot(p_ref[...].astype(BF16), wp_ref[...].astype(BF16), preferred_element_type=F32)
    out_ref[...] = x2_sc[j] + gate * proj


def _ple(x, y_slots, wts, g, p, wg, wp, *, layer):
    T, D = x.shape
    dp = p.shape[2]
    tm = _row_tile(T, 384)
    tn = 512
    nt = T // tm
    return pl.pallas_call(
        functools.partial(_ple_kernel, tn=tn),
        grid=(nt, D // tn),
        in_specs=[
            pl.BlockSpec((tm, D), lambda i, j: (i, 0)),
            pl.BlockSpec((tm, D), lambda i, j: (i, 0)),
            pl.BlockSpec((tm, D), lambda i, j: (i + nt, 0)),
            pl.BlockSpec((tm, TOP_K), lambda i, j: (i, 0)),
            _layer_vec(layer, D),
            pl.BlockSpec((None, tm, dp), lambda i, j: (layer, i, 0)),
            pl.BlockSpec((None, D, tn), lambda i, j: (layer, 0, j)),
            pl.BlockSpec((None, dp, tn), lambda i, j: (layer, 0, j)),
        ],
        out_specs=pl.BlockSpec((tm, tn), lambda i, j: (i, j)),
        out_shape=jax.ShapeDtypeStruct((T, D), F32),
        scratch_shapes=[pltpu.VMEM((D // tn, tm, tn), F32), pltpu.VMEM((tm, D), BF16)],
        compiler_params=_params(("parallel", "arbitrary")),
        name="ple",
    )(x, y_slots, y_slots, wts, g, p, wg, wp)


def kernel(x_prompt, x_sample, cache_k, cache_v, state_rnn_h, state_conv, p_prompt, p_sample, norm_mix, w_in, q_norm, k_norm, lambda_q1, lambda_k1, lambda_q2, lambda_k2, attn_out_norm, conv_w, conv_b, w_rg_a, b_rg_a, w_rg_x, b_rg_x, rg_lambda, rnn_out_norm, w_out, norm_ffn, w_router_group, b_router_group, w_router_expert, b_router_expert, w_exp_gate, w_exp_up, w_exp_down, norm_ple, w_ple_gate, w_ple_proj):
    bp, seq, D = x_prompt.shape
    nb, dseq, _ = x_sample.shape
    depth = w_in.shape[0]
    d_rnn = state_rnn_h.shape[-1]
    d_ple = p_prompt.shape[-1]
    assert bp == 1 and seq % CHUNK == 0 and dseq % 16 == 0 and seq % dseq == 0
    n_p = bp * seq
    n_s = nb * dseq
    T = n_p + n_s

    x = jnp.concatenate([x_prompt.reshape(n_p, D), x_sample.reshape(n_s, D)], axis=0)
    p_all = jnp.concatenate([p_prompt.reshape(depth, n_p, d_ple), p_sample.reshape(depth, n_s, d_ple)], axis=1)
    vec = lambda a: a.reshape(depth, 1, -1)
    conv_pad = jnp.pad(state_conv, ((0, 0), (0, 0), (SUBLANES - (CONV_W - 1), 0), (0, 0)))
    h0_s = state_rnn_h.reshape(depth, nb, 1, d_rnn)
    zero_h = jnp.zeros((1, bp, 1, d_rnn), F32)
    zero_conv = jnp.zeros((1, bp, SUBLANES, d_rnn), F32)
    pad_lanes = ROUTER_LANES - N_GROUPS - N_EXPERTS
    w_router = jnp.concatenate([w_router_group, w_router_expert, jnp.zeros((depth, D, pad_lanes), F32)], axis=-1)
    b_router = jnp.concatenate([b_router_group, b_router_expert, jnp.zeros((depth, pad_lanes), F32)], axis=-1)
    tb_p = _row_tile(seq, 256, SUBLANES)

    norm_mix_v, qg_v, kg_v = vec(norm_mix), vec(q_norm), vec(k_norm)
    lam_params = (vec(lambda_q1), vec(lambda_k1), vec(lambda_q2), vec(lambda_k2), vec(attn_out_norm))
    rnn_w = (conv_w, vec(conv_b), w_rg_a, vec(b_rg_a), w_rg_x, vec(b_rg_x), vec(rg_lambda), vec(rnn_out_norm))
    norm_ffn_v, b_router_v, norm_ple_v = vec(norm_ffn), vec(b_router), vec(norm_ple)

    ks, vs = [], []
    hp, cp, hs_, cs_ = [], [], [], []
    for l in range(depth):
        lam0 = _lambda_init(l)
        q, k, v, xr, gate, kb, vb = _in_proj(x, norm_mix_v, w_in, qg_v, kg_v, layer=l)
        ks.append(k)
        vs.append(v)
        o = jnp.zeros((T, D_ATTN), BF16)
        o = _attn_prompt(q, kb, vb, lam_params, qg_v, kg_v, o, layer=l, n_prompt=n_p, lam0=lam0)
        o = _attn_sample(q, kb, vb, cache_k, cache_v, lam_params, o, layer=l, n_prompt=n_p, rows=dseq, lam0=lam0)

        y = jnp.zeros((T, d_rnn), BF16)
        y, h_p, c_p = _rnn(xr, gate, zero_h, zero_conv, rnn_w, y, layer=l, state_layer=0,
                           row0=0, n_seq=bp, seq_len=seq, tb=tb_p)
        y, h_s, c_s = _rnn(xr, gate, h0_s, conv_pad, rnn_w, y, layer=l, state_layer=l,
                           row0=n_p, n_seq=nb, seq_len=dseq, tb=dseq)

        x = _out_proj(o, y, w_out, x, layer=l)

        hn, logits = _router(x, norm_ffn_v, w_router, b_router_v, layer=l)
        blk_exp, row_src, row_dst, n_used, wts = _routing_tables(logits, T)
        y_slots = _moe(hn, blk_exp, row_src, row_dst, n_used, w_exp_gate, w_exp_up, w_exp_down, layer=l)

        x = _ple(x, y_slots, wts, norm_ple_v, p_all, w_ple_gate, w_ple_proj, layer=l)

        hp.append(h_p[:, 0])
        cp.append(c_p[:, SUBLANES - (CONV_W - 1):])
        hs_.append(h_s[:, 0])
        cs_.append(c_s[:, SUBLANES - (CONV_W - 1):])

    k_p, k_s, v_p, v_s = _kv_layout(ks, vs, n_prompt=n_p, n_sample=n_s)
    return (x[:n_p].reshape(bp, seq, D), x[n_p:].reshape(nb, dseq, D),
            k_p.reshape(depth, bp, seq, N_HEADS, QK_DIM), v_p.reshape(depth, bp, seq, N_HEADS, V_DIM),
            jnp.stack(hp), jnp.stack(cp),
            k_s.reshape(depth, nb, dseq, N_HEADS, QK_DIM), v_s.reshape(depth, nb, dseq, N_HEADS, V_DIM),
            jnp.stack(hs_), jnp.stack(cs_))
```

```python
import functools
import math

import jax
import jax.numpy as jnp
from jax import lax
from jax.experimental import pallas as pl
from jax.experimental.pallas import tpu as pltpu

F32 = jnp.float32
BF16 = jnp.bfloat16

CHUNK = 64
N_HEADS = 8
HEAD_DIM = 64
QK_DIM = 2 * HEAD_DIM
V_DIM = 2 * HEAD_DIM
D_ATTN = N_HEADS * V_DIM
N_RNN_BLOCKS = 8
CONV_W = 4
RG_C = 8.0
N_GROUPS = 4
EXPERTS_PER_GROUP = 8
N_EXPERTS = N_GROUPS * EXPERTS_PER_GROUP
TOP_K = 2
EPS = 1e-6
NEG_INF = -1e30

LANES = 128
SUBLANES = 8
VMEM_LIMIT_BYTES = 56 * 1024 * 1024
MOE_ROWS = 256
ROUTER_LANES = 128
ROW_DMA_UNROLL = 8
COMBINE_CHUNKS = 3


def _lambda_init(layer):
    return 0.8 - 0.6 * math.exp(-0.3 * layer)


def _row_tile(total, target, multiple=16):
    best = None
    for t in range(multiple, min(total, target) + 1, multiple):
        if total % t == 0:
            best = t
    assert best is not None, (total, target)
    return best


def _params(sem):
    return pltpu.CompilerParams(dimension_semantics=sem, vmem_limit_bytes=VMEM_LIMIT_BYTES)


def _layer_vec(l, n):
    return pl.BlockSpec((None, 1, n), lambda *_: (l, 0, 0))


def _rms(x, gain):
    ms = jnp.mean(x * x, axis=-1, keepdims=True)
    return x * lax.rsqrt(ms + EPS) * gain


def _half_norm(a, gain, lo):
    sq = a * a
    s_lo = jnp.sum(jnp.where(lo, sq, 0.0), axis=-1, keepdims=True)
    s_hi = jnp.sum(jnp.where(lo, 0.0, sq), axis=-1, keepdims=True)
    ms = jnp.where(lo, s_lo, s_hi) * (1.0 / HEAD_DIM)
    return a * lax.rsqrt(ms + EPS) * gain


def _in_proj_kernel(x_ref, g_ref, w_ref, qg_ref, kg_ref,
                    q_ref, k_ref, v_ref, xr_ref, gate_ref, kb_ref, vb_ref, hn_ref, *, steps_per_section, tn):
    j = pl.program_id(1)

    @pl.when(j == 0)
    def _():
        hn_ref[...] = _rms(x_ref[...], g_ref[...]).astype(BF16)

    acc = jnp.dot(hn_ref[...], w_ref[...].astype(BF16), preferred_element_type=F32)
    sec = j // steps_per_section
    lo = lax.broadcasted_iota(jnp.int32, (1, LANES), 1) < HEAD_DIM

    @pl.when(sec == 0)
    def _():
        for h in range(tn // LANES):
            sl = slice(h * LANES, (h + 1) * LANES)
            q_ref[:, sl] = (_half_norm(acc[:, sl], qg_ref[...], lo) * (HEAD_DIM ** -0.5)).astype(BF16)

    @pl.when(sec == 1)
    def _():
        for h in range(tn // LANES):
            sl = slice(h * LANES, (h + 1) * LANES)
            kn = _half_norm(acc[:, sl], kg_ref[...], lo)
            k_ref[:, sl] = kn
            kb_ref[:, sl] = kn.astype(BF16)

    @pl.when(sec == 2)
    def _():
        v_ref[...] = acc
        vb_ref[...] = acc.astype(BF16)

    @pl.when(sec == 3)
    def _():
        xr_ref[...] = acc

    @pl.when(sec == 4)
    def _():
        gate_ref[...] = acc


def _in_proj(x, g, w_in, qg, kg, *, layer):
    T, D = x.shape
    tm = _row_tile(T, 1056)
    tn = 512
    sec_w = D_ATTN
    sps = sec_w // tn
    n_steps = w_in.shape[2] // tn
    col = lambda sec: (lambda j: jnp.clip(j - sec * sps, 0, sps - 1))

    def out_spec(sec):
        return pl.BlockSpec((tm, tn), lambda i, j: (i, col(sec)(j)))

    flat = jax.ShapeDtypeStruct((T, sec_w), F32)
    flat_b = jax.ShapeDtypeStruct((T, sec_w), BF16)
    return pl.pallas_call(
        functools.partial(_in_proj_kernel, steps_per_section=sps, tn=tn),
        grid=(T // tm, n_steps),
        in_specs=[
            pl.BlockSpec((tm, D), lambda i, j: (i, 0), pipeline_mode=pl.Buffered(1)),
            _layer_vec(layer, D),
            pl.BlockSpec((None, D, tn), lambda i, j: (layer, 0, j)),
            _layer_vec(layer, LANES),
            _layer_vec(layer, LANES),
        ],
        out_specs=[out_spec(s) for s in (0, 1, 2, 3, 4, 1, 2)],
        out_shape=[flat_b, flat, flat, flat, flat, flat_b, flat_b],
        scratch_shapes=[pltpu.VMEM((tm, D), BF16)],
        compiler_params=_params(("parallel", "arbitrary")),
        name="in_proj",
    )(x, g, w_in, qg, kg)


def _kv_layout_kernel(*refs, depth, prompt_tiles):
    srcs = refs[:2 * depth]
    kp_ref, ks_ref, vp_ref, vs_ref = refs[2 * depth:]
    l = pl.program_id(0)
    i = pl.program_id(1)

    def spread(src, dst):
        for h in range(N_HEADS):
            dst[:, h, :] = src[:, h * LANES:(h + 1) * LANES]

    for d in range(depth):
        @pl.when(jnp.logical_and(l == d, i < prompt_tiles))
        def _():
            spread(srcs[d], kp_ref)
            spread(srcs[depth + d], vp_ref)

        @pl.when(jnp.logical_and(l == d, i >= prompt_tiles))
        def _():
            spread(srcs[d], ks_ref)
            spread(srcs[depth + d], vs_ref)


def _kv_layout(ks, vs, *, n_prompt, n_sample):
    depth = len(ks)
    T, width = ks[0].shape
    tr = math.gcd(n_prompt, n_sample)
    nt, np_t, ns_t = T // tr, n_prompt // tr, n_sample // tr

    def src_spec(d):
        return pl.BlockSpec((tr, width), lambda l, i: (jnp.where(l < d, 0, jnp.where(l > d, nt - 1, i)), 0))

    blk = (None, tr, N_HEADS, LANES)
    prompt_spec = pl.BlockSpec(blk, lambda l, i: (l, jnp.minimum(i, np_t - 1), 0, 0))
    sample_spec = pl.BlockSpec(blk, lambda l, i: (l, jnp.clip(i - np_t, 0, ns_t - 1), 0, 0))
    shape = lambda n: jax.ShapeDtypeStruct((depth, n, N_HEADS, LANES), F32)
    return pl.pallas_call(
        functools.partial(_kv_layout_kernel, depth=depth, prompt_tiles=np_t),
        grid=(depth, nt),
        in_specs=[src_spec(d) for d in range(depth)] * 2,
        out_specs=[prompt_spec, sample_spec, prompt_spec, sample_spec],
        out_shape=[shape(n_prompt), shape(n_sample), shape(n_prompt), shape(n_sample)],
        compiler_params=_params(("arbitrary", "arbitrary")),
        name="kv_layout",
    )(*ks, *vs)


def _lambda_value(lq1_ref, lk1_ref, lq2_ref, lk2_ref, lam0):
    s1 = jnp.sum(lq1_ref[...] * lk1_ref[...], axis=-1, keepdims=True)
    s2 = jnp.sum(lq2_ref[...] * lk2_ref[...], axis=-1, keepdims=True)
    return jnp.exp(s1) - jnp.exp(s2) + lam0


def _split_q(q):
    lo = lax.broadcasted_iota(jnp.int32, q.shape, 1) < HEAD_DIM
    zero = jnp.zeros_like(q)
    return jnp.concatenate([jnp.where(lo, q, zero), jnp.where(lo, zero, q)], axis=0)


def _scores(q2, kb):
    return lax.dot_general(q2, kb, (((1,), (1,)), ((), ())), preferred_element_type=F32)


def _finish_heads(acc, l, lam, gout, rows, lam0):
    o = acc[:rows] / l[:rows] - lam * (acc[rows:] / l[rows:])
    return _rms(o, gout) * (1.0 - lam0)


def _attn_prompt_kernel(q_ref, k_ref, v_ref, lq1_ref, lk1_ref, lq2_ref, lk2_ref, gout_ref, o_init_ref,
                        o_ref, q2_sc, m_sc, acc_sc, *, bq, lam0):
    del o_init_ref
    i = pl.program_id(1)
    bk = bq
    q2_sc[...] = _split_q(q_ref[...])
    m_sc[...] = jnp.full(m_sc.shape, NEG_INF, F32)
    acc_sc[...] = jnp.zeros(acc_sc.shape, F32)
    ones = jnp.ones((bk, LANES), BF16)

    def update(off, masked):
        kb = k_ref[pl.ds(off, bk), :]
        vb = v_ref[pl.ds(off, bk), :]
        s = _scores(q2_sc[...], kb)
        if masked:
            q_chunk = (lax.broadcasted_iota(jnp.int32, s.shape, 0) % bq) // CHUNK
            k_chunk = lax.broadcasted_iota(jnp.int32, s.shape, 1) // CHUNK
            s = jnp.where(k_chunk <= q_chunk, s, NEG_INF)
        m_old = m_sc[...]
        m_new = jnp.maximum(m_old, jnp.max(s, axis=-1, keepdims=True))
        pr = jnp.exp(s - jnp.concatenate([m_new] * (bk // LANES), axis=1))
        alpha = jnp.exp(m_old - m_new)
        pv = jnp.dot(pr.astype(BF16), jnp.concatenate([vb, ones], axis=1), preferred_element_type=F32)
        acc_sc[...] = jnp.concatenate([alpha, alpha], axis=1) * acc_sc[...] + pv
        m_sc[...] = m_new

    def body(j, carry):
        update(pl.multiple_of(j * bk, bk), False)
        return carry

    lax.fori_loop(0, i, body, 0)
    update(pl.multiple_of(i * bk, bk), True)
    lam = _lambda_value(lq1_ref, lk1_ref, lq2_ref, lk2_ref, lam0)
    acc = acc_sc[...]
    o_ref[...] = _finish_heads(acc[:, :LANES], acc[:, LANES:], lam, gout_ref[...], bq, lam0).astype(o_ref.dtype)


def _score_bound(qg_ref, kg_ref):
    gq = jnp.max(jnp.abs(qg_ref[...]), axis=-1, keepdims=True)
    gk = jnp.max(jnp.abs(kg_ref[...]), axis=-1, keepdims=True)
    return (HEAD_DIM ** 0.5) * gq * gk


def _attn_prompt_shift_kernel(q_ref, k_ref, v_ref, lq1_ref, lk1_ref, lq2_ref, lk2_ref, gout_ref, qg_ref, kg_ref,
                              o_init_ref, o_ref, q2_sc, acc_sc, *, bq, lam0):
    del o_init_ref
    i = pl.program_id(1)
    bk = bq
    q2_sc[...] = _split_q(q_ref[...])
    acc_sc[...] = jnp.zeros(acc_sc.shape, F32)
    ones = jnp.ones((bk, LANES), BF16)
    shift = _score_bound(qg_ref, kg_ref)

    def update(off, masked):
        kb = k_ref[pl.ds(off, bk), :]
        vb = v_ref[pl.ds(off, bk), :]
        s = _scores(q2_sc[...], kb) - shift
        if masked:
            q_chunk = (lax.broadcasted_iota(jnp.int32, s.shape, 0) % bq) // CHUNK
            k_chunk = lax.broadcasted_iota(jnp.int32, s.shape, 1) // CHUNK
            s = jnp.where(k_chunk <= q_chunk, s, NEG_INF)
        pr = jnp.exp(s).astype(BF16)
        acc_sc[...] += jnp.dot(pr, jnp.concatenate([vb, ones], axis=1), preferred_element_type=F32)

    def body(j, carry):
        update(pl.multiple_of(j * bk, bk), False)
        return carry

    lax.fori_loop(0, i, body, 0)
    update(pl.multiple_of(i * bk, bk), True)
    lam = _lambda_value(lq1_ref, lk1_ref, lq2_ref, lk2_ref, lam0)
    acc = acc_sc[...]
    o_ref[...] = _finish_heads(acc[:, :LANES], acc[:, LANES:], lam, gout_ref[...], bq, lam0).astype(o_ref.dtype)


MAX_FIXED_SHIFT = 30.0


def _attn_prompt(q, kb, vb, lam_params, qg, kg, o_init, *, layer, n_prompt, lam0):
    bq = _row_tile(n_prompt, 1024, CHUNK)
    common_in = [
        pl.BlockSpec((bq, LANES), lambda h, i: (i, h)),
        pl.BlockSpec((n_prompt, LANES), lambda h, i: (0, h)),
        pl.BlockSpec((n_prompt, LANES), lambda h, i: (0, h)),
        _layer_vec(layer, HEAD_DIM), _layer_vec(layer, HEAD_DIM), _layer_vec(layer, HEAD_DIM),
        _layer_vec(layer, HEAD_DIM), _layer_vec(layer, V_DIM),
    ]
    common = dict(
        grid=(N_HEADS, n_prompt // bq),
        out_specs=pl.BlockSpec((bq, LANES), lambda h, i: (i, h)),
        out_shape=jax.ShapeDtypeStruct(o_init.shape, o_init.dtype),
        compiler_params=_params(("parallel", "arbitrary")),
    )
    q2_scratch = pltpu.VMEM((2 * bq, LANES), BF16)
    acc_scratch = pltpu.VMEM((2 * bq, 2 * LANES), F32)

    def streaming(o):
        return pl.pallas_call(
            functools.partial(_attn_prompt_kernel, bq=bq, lam0=lam0),
            in_specs=common_in + [pl.BlockSpec(memory_space=pl.ANY)],
            scratch_shapes=[q2_scratch, pltpu.VMEM((2 * bq, LANES), F32), acc_scratch],
            input_output_aliases={8: 0},
            name="attn_prompt",
            **common,
        )(q, kb, vb, *lam_params, o)

    def fixed_shift(o):
        return pl.pallas_call(
            functools.partial(_attn_prompt_shift_kernel, bq=bq, lam0=lam0),
            in_specs=common_in + [_layer_vec(layer, LANES), _layer_vec(layer, LANES),
                                  pl.BlockSpec(memory_space=pl.ANY)],
            scratch_shapes=[q2_scratch, acc_scratch],
            input_output_aliases={10: 0},
            name="attn_prompt_shift",
            **common,
        )(q, kb, vb, *lam_params, qg, kg, o)

    bound = (HEAD_DIM ** 0.5) * jnp.max(jnp.abs(qg[layer])) * jnp.max(jnp.abs(kg[layer]))
    return lax.cond(bound <= MAX_FIXED_SHIFT, fixed_shift, streaming, o_init)


def _attn_sample_kernel(q_ref, k_ref, v_ref, ck_ref, cv_ref, lq1_ref, lk1_ref, lq2_ref, lk2_ref, gout_ref,
                        o_init_ref, o_ref, *, rows, lam0):
    del o_init_ref
    lam = _lambda_value(lq1_ref, lk1_ref, lq2_ref, lk2_ref, lam0)
    for h in range(N_HEADS):
        sl = slice(h * LANES, (h + 1) * LANES)
        q2 = _split_q(q_ref[:, sl])
        s_c = _scores(q2, ck_ref[:, h, :].astype(BF16))
        s_n = _scores(q2, k_ref[:, sl])
        m = jnp.maximum(jnp.max(s_c, axis=-1, keepdims=True), jnp.max(s_n, axis=-1, keepdims=True))
        p_c = jnp.exp(s_c - m)
        p_n = jnp.exp(s_n - m)
        l = jnp.sum(p_c, axis=-1, keepdims=True) + jnp.sum(p_n, axis=-1, keepdims=True)
        acc = (jnp.dot(p_c.astype(BF16), cv_ref[:, h, :].astype(BF16), preferred_element_type=F32)
               + jnp.dot(p_n.astype(BF16), v_ref[:, sl], preferred_element_type=F32))
        o_ref[:, sl] = _finish_heads(acc, l, lam, gout_ref[...], rows, lam0).astype(o_ref.dtype)


def _attn_sample(q, kb, vb, cache_k, cache_v, lam_params, o_init, *, layer, n_prompt, rows, lam0):
    n_streams, past = cache_k.shape[1], cache_k.shape[2]
    base = n_prompt // rows
    row_spec = pl.BlockSpec((rows, D_ATTN), lambda b: (base + b, 0))
    cache_spec = pl.BlockSpec((None, None, past, N_HEADS, LANES), lambda b: (layer, b, 0, 0, 0))
    return pl.pallas_call(
        functools.partial(_attn_sample_kernel, rows=rows, lam0=lam0),
        grid=(n_streams,),
        in_specs=[row_spec, row_spec, row_spec, cache_spec, cache_spec,
                  _layer_vec(layer, HEAD_DIM), _layer_vec(layer, HEAD_DIM), _layer_vec(layer, HEAD_DIM),
                  _layer_vec(layer, HEAD_DIM), _layer_vec(layer, V_DIM),
                  pl.BlockSpec(memory_space=pl.ANY)],
        out_specs=row_spec,
        out_shape=jax.ShapeDtypeStruct(o_init.shape, o_init.dtype),
        input_output_aliases={10: 0},
        compiler_params=_params(("parallel",)),
        name="attn_sample",
    )(q, kb, vb, cache_k, cache_v, *lam_params, o_init)


def _gelu_tanh(x):
    return 0.5 * x * (1.0 + jnp.tanh(math.sqrt(2.0 / math.pi) * (x + 0.044715 * (x * x * x))))


def _rnn_kernel(xr_ref, gate_ref, h0_ref, cbuf_ref, cw_ref, cb_ref, wa_ref, ba_ref, wx_ref, bx_ref,
                lam_ref, gn_ref, y_init_ref, y_ref, hlast_ref, ctail_ref, h_sc, tail_sc, xc_sc, *, tb):
    del y_init_ref
    t = pl.program_id(1)

    @pl.when(t == 0)
    def _():
        h_sc[...] = h0_ref[...]
        tail_sc[...] = cbuf_ref[...]

    x = xr_ref[...]
    cw = cw_ref[...]
    cb = cb_ref[...]

    def conv(window, rows):
        acc = cb + window * cw[CONV_W - 1:CONV_W]
        for back in range(1, CONV_W):
            acc = acc + pltpu.roll(window, back, axis=0) * cw[CONV_W - 1 - back:CONV_W - back]
        return acc[-rows:]

    xc_sc[...] = conv(x, tb)
    head = jnp.concatenate([tail_sc[...], x[:SUBLANES]], axis=0)
    xc_sc[0:SUBLANES, :] = conv(head, SUBLANES)
    tail_sc[...] = x[tb - SUBLANES:]
    ctail_ref[...] = x[tb - SUBLANES:]
    xc = xc_sc[...]

    xb = xc.astype(BF16)
    r_parts, i_parts = [], []
    for n in range(N_RNN_BLOCKS):
        sl = slice(n * LANES, (n + 1) * LANES)
        r_parts.append(jnp.dot(xb[:, sl], wa_ref[n].astype(BF16), preferred_element_type=F32))
        i_parts.append(jnp.dot(xb[:, sl], wx_ref[n].astype(BF16), preferred_element_type=F32))
    r = jax.nn.sigmoid(jnp.concatenate(r_parts, axis=-1) + ba_ref[...])
    ig = jax.nn.sigmoid(jnp.concatenate(i_parts, axis=-1) + bx_ref[...])
    neg_lam = -lam_ref[...]
    softplus = jnp.maximum(neg_lam, 0.0) + jnp.log1p(jnp.exp(-jnp.abs(neg_lam)))
    log_a = (-RG_C * r) * softplus
    a = jnp.exp(log_a)
    u = jnp.sqrt(1.0 - a * a) * (ig * xc)

    row = lax.broadcasted_iota(jnp.int32, (tb, 1), 0)
    d = 1
    while d < tb:
        keep = row >= d
        u = jnp.where(keep, a * pltpu.roll(u, d, axis=0) + u, u)
        a = jnp.where(keep, a * pltpu.roll(a, d, axis=0), a)
        d *= 2
    hs = u + a * h_sc[...]
    h_sc[...] = hs[tb - 1:tb]
    hlast_ref[...] = hs[tb - 1:tb]

    y_ref[...] = _rms(hs * _gelu_tanh(gate_ref[...]), gn_ref[...]).astype(y_ref.dtype)


def _rnn(xr, gate, h0, cbuf, weights, y_init, *, layer, state_layer, row0, n_seq, seq_len, tb):
    C = xr.shape[1]
    nb = seq_len // tb
    base = row0 // tb
    row_spec = pl.BlockSpec((tb, C), lambda b, t: (base + b * nb + t, 0))
    blk = lambda: pl.BlockSpec((None, N_RNN_BLOCKS, LANES, LANES), lambda b, t: (layer, 0, 0, 0))
    return pl.pallas_call(
        functools.partial(_rnn_kernel, tb=tb),
        grid=(n_seq, nb),
        in_specs=[row_spec, row_spec,
                  pl.BlockSpec((None, None, 1, C), lambda b, t: (state_layer, b, 0, 0)),
                  pl.BlockSpec((None, None, SUBLANES, C), lambda b, t: (state_layer, b, 0, 0)),
                  pl.BlockSpec((None, CONV_W, C), lambda b, t: (layer, 0, 0)), _layer_vec(layer, C),
                  blk(), _layer_vec(layer, C), blk(), _layer_vec(layer, C), _layer_vec(layer, C),
                  _layer_vec(layer, C),
                  pl.BlockSpec(memory_space=pl.ANY)],
        out_specs=[row_spec,
                   pl.BlockSpec((None, 1, C), lambda b, t: (b, 0, 0)),
                   pl.BlockSpec((None, SUBLANES, C), lambda b, t: (b, 0, 0))],
        out_shape=[jax.ShapeDtypeStruct(y_init.shape, y_init.dtype),
                   jax.ShapeDtypeStruct((n_seq, 1, C), F32),
                   jax.ShapeDtypeStruct((n_seq, SUBLANES, C), F32)],
        scratch_shapes=[pltpu.VMEM((1, C), F32), pltpu.VMEM((SUBLANES, C), F32), pltpu.VMEM((tb, C), F32)],
        input_output_aliases={12: 0},
        compiler_params=_params(("arbitrary", "arbitrary")),
        name="rnn",
    )(xr, gate, h0, cbuf, *weights, y_init)


def _out_proj_kernel(o_ref, y_ref, wa_ref, wb_ref, x_ref, out_ref):
    acc = jnp.dot(o_ref[...], wa_ref[...].astype(BF16), preferred_element_type=F32)
    acc = acc + jnp.dot(y_ref[...], wb_ref[...].astype(BF16), preferred_element_type=F32)
    out_ref[...] = x_ref[...] + acc


def _out_proj(o, y, w_out, x, *, layer):
    T, D = x.shape
    half = o.shape[1]
    tm = _row_tile(T, 1408)
    tn = 512
    return pl.pallas_call(
        _out_proj_kernel,
        grid=(T // tm, D // tn),
        in_specs=[
            pl.BlockSpec((tm, half), lambda i, j: (i, 0)),
            pl.BlockSpec((tm, half), lambda i, j: (i, 0)),
            pl.BlockSpec((None, half, tn), lambda i, j: (layer, 0, j)),
            pl.BlockSpec((None, half, tn), lambda i, j: (layer, 1, j)),
            pl.BlockSpec((tm, tn), lambda i, j: (i, j)),
        ],
        out_specs=pl.BlockSpec((tm, tn), lambda i, j: (i, j)),
        out_shape=jax.ShapeDtypeStruct((T, D), F32),
        compiler_params=_params(("parallel", "arbitrary")),
        name="out_proj",
    )(o, y, w_out, w_out, x)


def _router_kernel(x_ref, g_ref, w_ref, b_ref, hn_ref, logit_ref):
    hn = _rms(x_ref[...], g_ref[...])
    hn_ref[...] = hn
    logit_ref[...] = jnp.dot(hn.astype(BF16), w_ref[...].astype(BF16), preferred_element_type=F32) + b_ref[...]


def _router(x, g, w, b, *, layer):
    T, D = x.shape
    tm = _row_tile(T, 768)
    return pl.pallas_call(
        _router_kernel,
        grid=(T // tm,),
        in_specs=[pl.BlockSpec((tm, D), lambda i: (i, 0)),
                  _layer_vec(layer, D),
                  pl.BlockSpec((None, D, ROUTER_LANES), lambda i: (layer, 0, 0)),
                  _layer_vec(layer, ROUTER_LANES)],
        out_specs=[pl.BlockSpec((tm, D), lambda i: (i, 0)),
                   pl.BlockSpec((tm, ROUTER_LANES), lambda i: (i, 0))],
        out_shape=[jax.ShapeDtypeStruct((T, D), F32), jax.ShapeDtypeStruct((T, ROUTER_LANES), F32)],
        compiler_params=_params(("parallel",)),
        name="router",
    )(x, g, w, b)


def _top1(x):
    val = jnp.max(x, axis=-1, keepdims=True)
    n = x.shape[-1]
    idx = jnp.min(jnp.where(x == val, lax.broadcasted_iota(jnp.int32, x.shape, x.ndim - 1), n), axis=-1,
                  keepdims=True)
    return val, idx


def _routing_tables(logits, n_tokens):
    T = n_tokens
    gp = jax.nn.softmax(logits[:, :N_GROUPS], axis=-1)
    g_val, g_idx = _top1(gp)
    el = logits[:, N_GROUPS:N_GROUPS + N_EXPERTS].reshape(T, N_GROUPS, EXPERTS_PER_GROUP)
    group = lax.broadcasted_iota(jnp.int32, el.shape, 1)
    el_g = jnp.sum(jnp.where(group == g_idx[:, :, None], el, 0.0), axis=1)
    v0, i0 = _top1(el_g)
    lane = lax.broadcasted_iota(jnp.int32, el_g.shape, 1)
    v1, i1 = _top1(jnp.where(lane == i0, -jnp.inf, el_g))
    e_val = jnp.concatenate([v0, v1], axis=-1)
    e_idx = jnp.concatenate([i0, i1], axis=-1)
    wts = g_val * jax.nn.softmax(e_val, axis=-1)
    eid = (g_idx * EXPERTS_PER_GROUP + e_idx).astype(jnp.int32)

    A = T * TOP_K
    B = MOE_ROWS
    n_blocks = -(-(A + N_EXPERTS * (B - 1)) // B)
    n_rows = n_blocks * B
    eid_f = eid.reshape(A)
    onehot = (eid_f[:, None] == jnp.arange(N_EXPERTS, dtype=jnp.int32)[None, :]).astype(jnp.int32)
    csum = jnp.cumsum(onehot, axis=0)
    rank = jnp.sum(onehot * csum, axis=1) - 1
    counts = csum[-1]
    padded = (counts + B - 1) // B * B
    pad_end = jnp.cumsum(padded)
    pad_start = pad_end - padded
    dest = jnp.sum(onehot * pad_start[None, :], axis=1) + rank
    row_a = jnp.full((n_rows,), -1, jnp.int32).at[dest].set(jnp.arange(A, dtype=jnp.int32))
    tok = jnp.maximum(row_a, 0) // TOP_K
    row_src = tok
    rows = jnp.arange(n_rows, dtype=jnp.int32)
    dump = TOP_K * T + ((rows // B) % 2) * B + rows % B
    row_dst = jnp.where(row_a < 0, dump, (row_a % TOP_K) * T + tok)
    blk_exp = jnp.minimum(
        jnp.sum((pad_end[None, :] <= (jnp.arange(n_blocks, dtype=jnp.int32) * B)[:, None]).astype(jnp.int32), axis=1),
        N_EXPERTS - 1).astype(jnp.int32)
    n_used = (pad_end[-1] // B).astype(jnp.int32).reshape(1)
    n_real = jnp.sum((row_a >= 0).reshape(n_blocks, B).astype(jnp.int32), axis=1)
    groups = (n_real + ROW_DMA_UNROLL - 1) // ROW_DMA_UNROLL
    return blk_exp, row_src, row_dst, n_used, groups, wts


def _moe_kernel(blk_ref, src_ref, dst_ref, nused_ref, grp_ref, hn_ref, w1_ref, w3_ref, w2_ref,
                y_ref, xg_sc, yo_sc, gsem, ssem, *, n_blocks):
    del blk_ref
    B = MOE_ROWS
    n_real = y_ref.shape[0] - 2 * B
    i = pl.program_id(0)
    n_used = nused_ref[0]
    slot = i % 2

    def start_gather(blk, s):
        def body(g, c):
            for u in range(ROW_DMA_UNROLL):
                r = g * ROW_DMA_UNROLL + u
                src = src_ref[blk * B + r]
                pltpu.make_async_copy(hn_ref.at[pl.ds(src, 1)], xg_sc.at[s, pl.ds(r, 1)],
                                      gsem.at[s]).start(priority=u % 2)
            return c
        lax.fori_loop(0, grp_ref[blk], body, 0)

    def wait_gather(blk, s):
        def body(g, c):
            pltpu.make_async_copy(hn_ref.at[pl.ds(0, ROW_DMA_UNROLL)], xg_sc.at[s, pl.ds(0, ROW_DMA_UNROLL)],
                                  gsem.at[s]).wait()
            return c
        lax.fori_loop(0, grp_ref[blk], body, 0)

    def start_scatter(blk, s):
        def body(g, c):
            for u in range(ROW_DMA_UNROLL):
                r = g * ROW_DMA_UNROLL + u
                dst = dst_ref[blk * B + r]
                pltpu.make_async_copy(yo_sc.at[s, pl.ds(r, 1)], y_ref.at[pl.ds(dst, 1)],
                                      ssem.at[s]).start(priority=u % 2)
            return c
        lax.fori_loop(0, grp_ref[blk], body, 0)

    def wait_scatter(blk, s):
        def body(g, c):
            pltpu.make_async_copy(yo_sc.at[s, pl.ds(0, ROW_DMA_UNROLL)], y_ref.at[pl.ds(0, ROW_DMA_UNROLL)],
                                  ssem.at[s]).wait()
            return c
        lax.fori_loop(0, grp_ref[blk], body, 0)

    @pl.when(i == 0)
    def _():
        xg_sc[...] = jnp.zeros(xg_sc.shape, F32)
        start_gather(0, 0)
        yo_sc[0] = jnp.zeros(yo_sc.shape[1:], F32)
        for half in range(2):
            fill = pltpu.make_async_copy(yo_sc.at[0], y_ref.at[pl.ds(n_real + half * B, B)], ssem.at[0])
            fill.start()
            fill.wait()

    @pl.when(i + 1 < n_used)
    def _():
        start_gather(i + 1, 1 - slot)

    @pl.when(i < n_used)
    def _():
        wait_gather(i, slot)

        @pl.when(i >= 2)
        def _():
            wait_scatter(i - 2, slot)

        xb = xg_sc[slot].astype(BF16)
        h1 = jnp.dot(xb, w1_ref[...].astype(BF16), preferred_element_type=F32)
        h3 = jnp.dot(xb, w3_ref[...].astype(BF16), preferred_element_type=F32)
        h = (h1 * jax.nn.sigmoid(h1) * h3).astype(BF16)
        yo_sc[slot] = jnp.dot(h, w2_ref[...].astype(BF16), preferred_element_type=F32)
        start_scatter(i, slot)

    @pl.when(i == n_blocks - 1)
    def _():
        wait_scatter(n_used - 1, (n_used - 1) % 2)

        @pl.when(n_used >= 2)
        def _():
            wait_scatter(n_used - 2, n_used % 2)


def _moe(hn, blk_exp, row_src, row_dst, n_used, groups, w1, w3, w2, *, layer):
    T, D = hn.shape
    B = MOE_ROWS
    n_blocks = blk_exp.shape[0]
    de = w1.shape[3]
    grid_spec = pltpu.PrefetchScalarGridSpec(
        num_scalar_prefetch=5,
        grid=(n_blocks,),
        in_specs=[
            pl.BlockSpec(memory_space=pl.ANY),
            pl.BlockSpec((None, None, D, de), lambda i, blk, src, dst, nu, grp: (layer, blk[i], 0, 0)),
            pl.BlockSpec((None, None, D, de), lambda i, blk, src, dst, nu, grp: (layer, blk[i], 0, 0)),
            pl.BlockSpec((None, None, de, D), lambda i, blk, src, dst, nu, grp: (layer, blk[i], 0, 0)),
        ],
        out_specs=pl.BlockSpec(memory_space=pl.ANY),
        scratch_shapes=[
            pltpu.VMEM((2, B, D), F32),
            pltpu.VMEM((2, B, D), F32),
            pltpu.SemaphoreType.DMA((2,)),
            pltpu.SemaphoreType.DMA((2,)),
        ],
    )
    return pl.pallas_call(
        functools.partial(_moe_kernel, n_blocks=n_blocks),
        grid_spec=grid_spec,
        out_shape=jax.ShapeDtypeStruct((TOP_K * T + 2 * B, D), F32),
        compiler_params=_params(("arbitrary",)),
        name="moe",
    )(blk_exp, row_src, row_dst, n_used, groups, hn, w1, w3, w2)


def _ple_kernel(x_ref, y0_ref, y1_ref, wt_ref, g_ref, p_ref, wg_ref, wp_ref, out_ref, x2_sc, hn_sc, *, tn):
    j = pl.program_id(1)

    @pl.when(j == 0)
    def _():
        tm = x_ref.shape[0]
        rc = tm // COMBINE_CHUNKS
        for c0 in range(0, tm, rc):
            rows = slice(c0, c0 + rc)
            wt = wt_ref[rows, :]
            x2 = x_ref[rows, :] + (wt[:, 0:1] * y0_ref[rows, :] + wt[:, 1:2] * y1_ref[rows, :])
            hn_sc[rows, :] = _rms(x2, g_ref[...]).astype(BF16)
            for c in range(x2_sc.shape[0]):
                x2_sc[c, rows, :] = x2[:, c * tn:(c + 1) * tn]

    gate = jax.nn.sigmoid(jnp.dot(hn_sc[...], wg_ref[...].astype(BF16), preferred_element_type=F32))
    proj = jnp.dot(p_ref[...].astype(BF16), wp_ref[...].astype(BF16), preferred_element_type=F32)
    out_ref[...] = x2_sc[j] + gate * proj


def _ple(x, y_slots, wts, g, p, wg, wp, *, layer):
    T, D = x.shape
    dp = p.shape[2]
    tm = _row_tile(T, 528, 16 * COMBINE_CHUNKS)
    tn = 512
    nt = T // tm
    return pl.pallas_call(
        functools.partial(_ple_kernel, tn=tn),
        grid=(nt, D // tn),
        in_specs=[
            pl.BlockSpec((tm, D), lambda i, j: (i, 0)),
            pl.BlockSpec((tm, D), lambda i, j: (i, 0)),
            pl.BlockSpec((tm, D), lambda i, j: (i + nt, 0)),
            pl.BlockSpec((tm, TOP_K), lambda i, j: (i, 0)),
            _layer_vec(layer, D),
            pl.BlockSpec((None, tm, dp), lambda i, j: (layer, i, 0)),
            pl.BlockSpec((None, D, tn), lambda i, j: (layer, 0, j)),
            pl.BlockSpec((None, dp, tn), lambda i, j: (layer, 0, j)),
        ],
        out_specs=pl.BlockSpec((tm, tn), lambda i, j: (i, j)),
        out_shape=jax.ShapeDtypeStruct((T, D), F32),
        scratch_shapes=[pltpu.VMEM((D // tn, tm, tn), F32), pltpu.VMEM((tm, D), BF16)],
        compiler_params=_params(("parallel", "arbitrary")),
        name="ple",
    )(x, y_slots, y_slots, wts, g, p, wg, wp)


def kernel(x_prompt, x_sample, cache_k, cache_v, state_rnn_h, state_conv, p_prompt, p_sample, norm_mix, w_in, q_norm, k_norm, lambda_q1, lambda_k1, lambda_q2, lambda_k2, attn_out_norm, conv_w, conv_b, w_rg_a, b_rg_a, w_rg_x, b_rg_x, rg_lambda, rnn_out_norm, w_out, norm_ffn, w_router_group, b_router_group, w_router_expert, b_router_expert, w_exp_gate, w_exp_up, w_exp_down, norm_ple, w_ple_gate, w_ple_proj):
    bp, seq, D = x_prompt.shape
    nb, dseq, _ = x_sample.shape
    depth = w_in.shape[0]
    d_rnn = state_rnn_h.shape[-1]
    d_ple = p_prompt.shape[-1]
    assert bp == 1 and seq % CHUNK == 0 and dseq % 16 == 0 and seq % dseq == 0
    n_p = bp * seq
    n_s = nb * dseq
    T = n_p + n_s

    x = jnp.concatenate([x_prompt.reshape(n_p, D), x_sample.reshape(n_s, D)], axis=0)
    p_all = jnp.concatenate([p_prompt.reshape(depth, n_p, d_ple), p_sample.reshape(depth, n_s, d_ple)], axis=1)
    vec = lambda a: a.reshape(depth, 1, -1)
    conv_pad = jnp.pad(state_conv, ((0, 0), (0, 0), (SUBLANES - (CONV_W - 1), 0), (0, 0)))
    h0_s = state_rnn_h.reshape(depth, nb, 1, d_rnn)
    zero_h = jnp.zeros((1, bp, 1, d_rnn), F32)
    zero_conv = jnp.zeros((1, bp, SUBLANES, d_rnn), F32)
    pad_lanes = ROUTER_LANES - N_GROUPS - N_EXPERTS
    w_router = jnp.concatenate([w_router_group, w_router_expert, jnp.zeros((depth, D, pad_lanes), F32)], axis=-1)
    b_router = jnp.concatenate([b_router_group, b_router_expert, jnp.zeros((depth, pad_lanes), F32)], axis=-1)
    tb_p = _row_tile(seq, 256, SUBLANES)

    norm_mix_v, qg_v, kg_v = vec(norm_mix), vec(q_norm), vec(k_norm)
    lam_params = (vec(lambda_q1), vec(lambda_k1), vec(lambda_q2), vec(lambda_k2), vec(attn_out_norm))
    rnn_w = (conv_w, vec(conv_b), w_rg_a, vec(b_rg_a), w_rg_x, vec(b_rg_x), vec(rg_lambda), vec(rnn_out_norm))
    norm_ffn_v, b_router_v, norm_ple_v = vec(norm_ffn), vec(b_router), vec(norm_ple)

    ks, vs = [], []
    hp, cp, hs_, cs_ = [], [], [], []
    for l in range(depth):
        lam0 = _lambda_init(l)
        q, k, v, xr, gate, kb, vb = _in_proj(x, norm_mix_v, w_in, qg_v, kg_v, layer=l)
        ks.append(k)
        vs.append(v)
        o = jnp.zeros((T, D_ATTN), BF16)
        o = _attn_prompt(q, kb, vb, lam_params, qg_v, kg_v, o, layer=l, n_prompt=n_p, lam0=lam0)
        o = _attn_sample(q, kb, vb, cache_k, cache_v, lam_params, o, layer=l, n_prompt=n_p, rows=dseq, lam0=lam0)

        y = jnp.zeros((T, d_rnn), BF16)
        y, h_p, c_p = _rnn(xr, gate, zero_h, zero_conv, rnn_w, y, layer=l, state_layer=0,
                           row0=0, n_seq=bp, seq_len=seq, tb=tb_p)
        y, h_s, c_s = _rnn(xr, gate, h0_s, conv_pad, rnn_w, y, layer=l, state_layer=l,
                           row0=n_p, n_seq=nb, seq_len=dseq, tb=dseq)

        x = _out_proj(o, y, w_out, x, layer=l)

        hn, logits = _router(x, norm_ffn_v, w_router, b_router_v, layer=l)
        blk_exp, row_src, row_dst, n_used, groups, wts = _routing_tables(logits, T)
        y_slots = _moe(hn, blk_exp, row_src, row_dst, n_used, groups, w_exp_gate, w_exp_up, w_exp_down, layer=l)

        x = _ple(x, y_slots, wts, norm_ple_v, p_all, w_ple_gate, w_ple_proj, layer=l)

        hp.append(h_p[:, 0])
        cp.append(c_p[:, SUBLANES - (CONV_W - 1):])
        hs_.append(h_s[:, 0])
        cs_.append(c_s[:, SUBLANES - (CONV_W - 1):])

    k_p, k_s, v_p, v_s = _kv_layout(ks, vs, n_prompt=n_p, n_sample=n_s)
    return (x[:n_p].reshape(bp, seq, D), x[n_p:].reshape(nb, dseq, D),
            k_p.reshape(depth, bp, seq, N_HEADS, QK_DIM), v_p.reshape(depth, bp, seq, N_HEADS, V_DIM),
            jnp.stack(hp), jnp.stack(cp),
            k_s.reshape(depth, nb, dseq, N_HEADS, QK_DIM), v_s.reshape(depth, nb, dseq, N_HEADS, V_DIM),
            jnp.stack(hs_), jnp.stack(cs_))
```

```python
import functools
import math

import jax
import jax.numpy as jnp
from jax import lax
from jax.experimental import pallas as pl
from jax.experimental.pallas import tpu as pltpu

F32 = jnp.float32
BF16 = jnp.bfloat16

CHUNK = 64
N_HEADS = 8
HEAD_DIM = 64
QK_DIM = 2 * HEAD_DIM
V_DIM = 2 * HEAD_DIM
D_ATTN = N_HEADS * V_DIM
N_RNN_BLOCKS = 8
CONV_W = 4
RG_C = 8.0
N_GROUPS = 4
EXPERTS_PER_GROUP = 8
N_EXPERTS = N_GROUPS * EXPERTS_PER_GROUP
TOP_K = 2
EPS = 1e-6
NEG_INF = -1e30

LANES = 128
SUBLANES = 8
VMEM_LIMIT_BYTES = 56 * 1024 * 1024
MOE_ROWS = 256
ROUTER_LANES = 128
ROW_DMA_UNROLL = 8
COMBINE_CHUNKS = 3


def _lambda_init(layer):
    return 0.8 - 0.6 * math.exp(-0.3 * layer)


def _row_tile(total, target, multiple=16):
    best = None
    for t in range(multiple, min(total, target) + 1, multiple):
        if total % t == 0:
            best = t
    assert best is not None, (total, target)
    return best


def _params(sem):
    return pltpu.CompilerParams(dimension_semantics=sem, vmem_limit_bytes=VMEM_LIMIT_BYTES)


def _layer_vec(l, n):
    return pl.BlockSpec((None, 1, n), lambda *_: (l, 0, 0))


def _rms(x, gain):
    ms = jnp.mean(x * x, axis=-1, keepdims=True)
    return x * lax.rsqrt(ms + EPS) * gain


def _half_norm(a, gain, lo):
    sq = a * a
    s_lo = jnp.sum(jnp.where(lo, sq, 0.0), axis=-1, keepdims=True)
    s_hi = jnp.sum(jnp.where(lo, 0.0, sq), axis=-1, keepdims=True)
    ms = jnp.where(lo, s_lo, s_hi) * (1.0 / HEAD_DIM)
    return a * lax.rsqrt(ms + EPS) * gain


def _in_proj_kernel(x_ref, g_ref, w_ref, qg_ref, kg_ref,
                    q_ref, k_ref, v_ref, xr_ref, gate_ref, kb_ref, vb_ref, hn_ref, *, steps_per_section, tn):
    j = pl.program_id(1)

    @pl.when(j == 0)
    def _():
        hn_ref[...] = _rms(x_ref[...], g_ref[...]).astype(BF16)

    acc = jnp.dot(hn_ref[...], w_ref[...].astype(BF16), preferred_element_type=F32)
    sec = j // steps_per_section
    lo = lax.broadcasted_iota(jnp.int32, (1, LANES), 1) < HEAD_DIM

    @pl.when(sec == 0)
    def _():
        for h in range(tn // LANES):
            sl = slice(h * LANES, (h + 1) * LANES)
            q_ref[:, sl] = (_half_norm(acc[:, sl], qg_ref[...], lo) * (HEAD_DIM ** -0.5)).astype(BF16)

    @pl.when(sec == 1)
    def _():
        for h in range(tn // LANES):
            sl = slice(h * LANES, (h + 1) * LANES)
            kn = _half_norm(acc[:, sl], kg_ref[...], lo)
            k_ref[:, sl] = kn
            kb_ref[:, sl] = kn.astype(BF16)

    @pl.when(sec == 2)
    def _():
        v_ref[...] = acc
        vb_ref[...] = acc.astype(BF16)

    @pl.when(sec == 3)
    def _():
        xr_ref[...] = acc

    @pl.when(sec == 4)
    def _():
        gate_ref[...] = acc


def _in_proj(x, g, w_in, qg, kg, *, layer):
    T, D = x.shape
    tm = _row_tile(T, 1056)
    tn = 512
    sec_w = D_ATTN
    sps = sec_w // tn
    n_steps = w_in.shape[2] // tn
    col = lambda sec: (lambda j: jnp.clip(j - sec * sps, 0, sps - 1))

    def out_spec(sec):
        return pl.BlockSpec((tm, tn), lambda i, j: (i, col(sec)(j)))

    flat = jax.ShapeDtypeStruct((T, sec_w), F32)
    flat_b = jax.ShapeDtypeStruct((T, sec_w), BF16)
    return pl.pallas_call(
        functools.partial(_in_proj_kernel, steps_per_section=sps, tn=tn),
        grid=(T // tm, n_steps),
        in_specs=[
            pl.BlockSpec((tm, D), lambda i, j: (i, 0), pipeline_mode=pl.Buffered(1)),
            _layer_vec(layer, D),
            pl.BlockSpec((None, D, tn), lambda i, j: (layer, 0, j)),
            _layer_vec(layer, LANES),
            _layer_vec(layer, LANES),
        ],
        out_specs=[out_spec(s) for s in (0, 1, 2, 3, 4, 1, 2)],
        out_shape=[flat_b, flat, flat, flat, flat, flat_b, flat_b],
        scratch_shapes=[pltpu.VMEM((tm, D), BF16)],
        compiler_params=_params(("parallel", "arbitrary")),
        name="in_proj",
    )(x, g, w_in, qg, kg)


def _kv_layout_kernel(*refs, depth, prompt_tiles):
    srcs = refs[:2 * depth]
    kp_ref, ks_ref, vp_ref, vs_ref = refs[2 * depth:]
    l = pl.program_id(0)
    i = pl.program_id(1)

    def spread(src, dst):
        for h in range(N_HEADS):
            dst[:, h, :] = src[:, h * LANES:(h + 1) * LANES]

    for d in range(depth):
        @pl.when(jnp.logical_and(l == d, i < prompt_tiles))
        def _():
            spread(srcs[d], kp_ref)
            spread(srcs[depth + d], vp_ref)

        @pl.when(jnp.logical_and(l == d, i >= prompt_tiles))
        def _():
            spread(srcs[d], ks_ref)
            spread(srcs[depth + d], vs_ref)


def _kv_layout(ks, vs, *, n_prompt, n_sample):
    depth = len(ks)
    T, width = ks[0].shape
    tr = math.gcd(n_prompt, n_sample)
    nt, np_t, ns_t = T // tr, n_prompt // tr, n_sample // tr

    def src_spec(d):
        return pl.BlockSpec((tr, width), lambda l, i: (jnp.where(l < d, 0, jnp.where(l > d, nt - 1, i)), 0))

    blk = (None, tr, N_HEADS, LANES)
    prompt_spec = pl.BlockSpec(blk, lambda l, i: (l, jnp.minimum(i, np_t - 1), 0, 0))
    sample_spec = pl.BlockSpec(blk, lambda l, i: (l, jnp.clip(i - np_t, 0, ns_t - 1), 0, 0))
    shape = lambda n: jax.ShapeDtypeStruct((depth, n, N_HEADS, LANES), F32)
    return pl.pallas_call(
        functools.partial(_kv_layout_kernel, depth=depth, prompt_tiles=np_t),
        grid=(depth, nt),
        in_specs=[src_spec(d) for d in range(depth)] * 2,
        out_specs=[prompt_spec, sample_spec, prompt_spec, sample_spec],
        out_shape=[shape(n_prompt), shape(n_sample), shape(n_prompt), shape(n_sample)],
        compiler_params=_params(("arbitrary", "arbitrary")),
        name="kv_layout",
    )(*ks, *vs)


def _lambda_value(lq1_ref, lk1_ref, lq2_ref, lk2_ref, lam0):
    s1 = jnp.sum(lq1_ref[...] * lk1_ref[...], axis=-1, keepdims=True)
    s2 = jnp.sum(lq2_ref[...] * lk2_ref[...], axis=-1, keepdims=True)
    return jnp.exp(s1) - jnp.exp(s2) + lam0


def _split_q(q):
    lo = lax.broadcasted_iota(jnp.int32, q.shape, 1) < HEAD_DIM
    zero = jnp.zeros_like(q)
    return jnp.concatenate([jnp.where(lo, q, zero), jnp.where(lo, zero, q)], axis=0)


def _scores(q2, kb):
    return lax.dot_general(q2, kb, (((1,), (1,)), ((), ())), preferred_element_type=F32)


def _finish_heads(acc, l, lam, gout, rows, lam0):
    o = acc[:rows] / l[:rows] - lam * (acc[rows:] / l[rows:])
    return _rms(o, gout) * (1.0 - lam0)


def _attn_prompt_kernel(q_ref, k_ref, v_ref, lq1_ref, lk1_ref, lq2_ref, lk2_ref, gout_ref, o_init_ref,
                        o_ref, q2_sc, m_sc, acc_sc, *, bq, lam0):
    del o_init_ref
    i = pl.program_id(1)
    bk = bq
    q2_sc[...] = _split_q(q_ref[...])
    m_sc[...] = jnp.full(m_sc.shape, NEG_INF, F32)
    acc_sc[...] = jnp.zeros(acc_sc.shape, F32)
    ones = jnp.ones((bk, LANES), BF16)

    def update(off, masked):
        kb = k_ref[pl.ds(off, bk), :]
        vb = v_ref[pl.ds(off, bk), :]
        s = _scores(q2_sc[...], kb)
        if masked:
            q_chunk = (lax.broadcasted_iota(jnp.int32, s.shape, 0) % bq) // CHUNK
            k_chunk = lax.broadcasted_iota(jnp.int32, s.shape, 1) // CHUNK
            s = jnp.where(k_chunk <= q_chunk, s, NEG_INF)
        m_old = m_sc[...]
        m_new = jnp.maximum(m_old, jnp.max(s, axis=-1, keepdims=True))
        pr = jnp.exp(s - jnp.concatenate([m_new] * (bk // LANES), axis=1))
        alpha = jnp.exp(m_old - m_new)
        pv = jnp.dot(pr.astype(BF16), jnp.concatenate([vb, ones], axis=1), preferred_element_type=F32)
        acc_sc[...] = jnp.concatenate([alpha, alpha], axis=1) * acc_sc[...] + pv
        m_sc[...] = m_new

    def body(j, carry):
        update(pl.multiple_of(j * bk, bk), False)
        return carry

    lax.fori_loop(0, i, body, 0)
    update(pl.multiple_of(i * bk, bk), True)
    lam = _lambda_value(lq1_ref, lk1_ref, lq2_ref, lk2_ref, lam0)
    acc = acc_sc[...]
    o_ref[...] = _finish_heads(acc[:, :LANES], acc[:, LANES:], lam, gout_ref[...], bq, lam0).astype(o_ref.dtype)


def _score_bound(qg_ref, kg_ref):
    gq = jnp.max(jnp.abs(qg_ref[...]), axis=-1, keepdims=True)
    gk = jnp.max(jnp.abs(kg_ref[...]), axis=-1, keepdims=True)
    return (HEAD_DIM ** 0.5) * gq * gk


def _attn_prompt_shift_kernel(q_ref, k_ref, v_ref, lq1_ref, lk1_ref, lq2_ref, lk2_ref, gout_ref, qg_ref, kg_ref,
                              o_init_ref, o_ref, q2_sc, acc_sc, *, bq, lam0):
    del o_init_ref
    i = pl.program_id(1)
    bk = bq
    hb = bq // 2
    for a in range(2):
        q2_sc[a * bq:(a + 1) * bq, :] = _split_q(q_ref[a * hb:(a + 1) * hb, :])
    acc_sc[...] = jnp.zeros(acc_sc.shape, F32)
    shift = _score_bound(qg_ref, kg_ref)

    def accumulate(rows, off, n_keys, mask_from_half):
        kb = k_ref[pl.ds(off, n_keys), :]
        vb = v_ref[pl.ds(off, n_keys), :]
        s = _scores(q2_sc[rows, :], kb) - shift
        if mask_from_half is not None:
            q_chunk = (lax.broadcasted_iota(jnp.int32, s.shape, 0) % hb + mask_from_half * hb) // CHUNK
            k_chunk = lax.broadcasted_iota(jnp.int32, s.shape, 1) // CHUNK
            s = jnp.where(k_chunk <= q_chunk, s, NEG_INF)
        pr = jnp.exp(s).astype(BF16)
        ones = jnp.ones((n_keys, LANES), BF16)
        acc_sc[rows, :] += jnp.dot(pr, jnp.concatenate([vb, ones], axis=1), preferred_element_type=F32)

    everything = slice(0, 2 * bq)

    def body(jp, carry):
        accumulate(everything, pl.multiple_of(2 * jp * bk, bk), bk, None)
        accumulate(everything, pl.multiple_of((2 * jp + 1) * bk, bk), bk, None)
        return carry

    lax.fori_loop(0, i // 2, body, 0)

    @pl.when(i % 2 == 1)
    def _():
        accumulate(everything, pl.multiple_of((i - 1) * bk, bk), bk, None)

    diag = pl.multiple_of(i * bk, bk)
    accumulate(slice(0, bq), diag, hb, 0)
    accumulate(slice(bq, 2 * bq), diag, bk, 1)
    lam = _lambda_value(lq1_ref, lk1_ref, lq2_ref, lk2_ref, lam0)
    for a in range(2):
        acc = acc_sc[a * bq:(a + 1) * bq, :]
        o_ref[a * hb:(a + 1) * hb, :] = _finish_heads(
            acc[:, :LANES], acc[:, LANES:], lam, gout_ref[...], hb, lam0).astype(o_ref.dtype)


MAX_FIXED_SHIFT = 30.0


def _attn_prompt(q, kb, vb, lam_params, qg, kg, o_init, *, layer, n_prompt, lam0):
    bq = _row_tile(n_prompt, 1024, 2 * LANES)
    common_in = [
        pl.BlockSpec((bq, LANES), lambda h, i: (i, h)),
        pl.BlockSpec((n_prompt, LANES), lambda h, i: (0, h)),
        pl.BlockSpec((n_prompt, LANES), lambda h, i: (0, h)),
        _layer_vec(layer, HEAD_DIM), _layer_vec(layer, HEAD_DIM), _layer_vec(layer, HEAD_DIM),
        _layer_vec(layer, HEAD_DIM), _layer_vec(layer, V_DIM),
    ]
    common = dict(
        grid=(N_HEADS, n_prompt // bq),
        out_specs=pl.BlockSpec((bq, LANES), lambda h, i: (i, h)),
        out_shape=jax.ShapeDtypeStruct(o_init.shape, o_init.dtype),
        compiler_params=_params(("parallel", "arbitrary")),
    )
    q2_scratch = pltpu.VMEM((2 * bq, LANES), BF16)
    acc_scratch = pltpu.VMEM((2 * bq, 2 * LANES), F32)

    def streaming(o):
        return pl.pallas_call(
            functools.partial(_attn_prompt_kernel, bq=bq, lam0=lam0),
            in_specs=common_in + [pl.BlockSpec(memory_space=pl.ANY)],
            scratch_shapes=[q2_scratch, pltpu.VMEM((2 * bq, LANES), F32), acc_scratch],
            input_output_aliases={8: 0},
            name="attn_prompt",
            **common,
        )(q, kb, vb, *lam_params, o)

    def fixed_shift(o):
        return pl.pallas_call(
            functools.partial(_attn_prompt_shift_kernel, bq=bq, lam0=lam0),
            in_specs=common_in + [_layer_vec(layer, LANES), _layer_vec(layer, LANES),
                                  pl.BlockSpec(memory_space=pl.ANY)],
            scratch_shapes=[q2_scratch, acc_scratch],
            input_output_aliases={10: 0},
            name="attn_prompt_shift",
            **common,
        )(q, kb, vb, *lam_params, qg, kg, o)

    bound = (HEAD_DIM ** 0.5) * jnp.max(jnp.abs(qg[layer])) * jnp.max(jnp.abs(kg[layer]))
    return lax.cond(bound <= MAX_FIXED_SHIFT, fixed_shift, streaming, o_init)


def _attn_sample_kernel(q_ref, k_ref, v_ref, ck_ref, cv_ref, lq1_ref, lk1_ref, lq2_ref, lk2_ref, gout_ref,
                        o_init_ref, o_ref, *, rows, lam0):
    del o_init_ref
    lam = _lambda_value(lq1_ref, lk1_ref, lq2_ref, lk2_ref, lam0)
    for h in range(N_HEADS):
        sl = slice(h * LANES, (h + 1) * LANES)
        q2 = _split_q(q_ref[:, sl])
        s_c = _scores(q2, ck_ref[:, h, :].astype(BF16))
        s_n = _scores(q2, k_ref[:, sl])
        m = jnp.maximum(jnp.max(s_c, axis=-1, keepdims=True), jnp.max(s_n, axis=-1, keepdims=True))
        p_c = jnp.exp(s_c - m)
        p_n = jnp.exp(s_n - m)
        l = jnp.sum(p_c, axis=-1, keepdims=True) + jnp.sum(p_n, axis=-1, keepdims=True)
        acc = (jnp.dot(p_c.astype(BF16), cv_ref[:, h, :].astype(BF16), preferred_element_type=F32)
               + jnp.dot(p_n.astype(BF16), v_ref[:, sl], preferred_element_type=F32))
        o_ref[:, sl] = _finish_heads(acc, l, lam, gout_ref[...], rows, lam0).astype(o_ref.dtype)


def _attn_sample(q, kb, vb, cache_k, cache_v, lam_params, o_init, *, layer, n_prompt, rows, lam0):
    n_streams, past = cache_k.shape[1], cache_k.shape[2]
    base = n_prompt // rows
    row_spec = pl.BlockSpec((rows, D_ATTN), lambda b: (base + b, 0))
    cache_spec = pl.BlockSpec((None, None, past, N_HEADS, LANES), lambda b: (layer, b, 0, 0, 0))
    return pl.pallas_call(
        functools.partial(_attn_sample_kernel, rows=rows, lam0=lam0),
        grid=(n_streams,),
        in_specs=[row_spec, row_spec, row_spec, cache_spec, cache_spec,
                  _layer_vec(layer, HEAD_DIM), _layer_vec(layer, HEAD_DIM), _layer_vec(layer, HEAD_DIM),
                  _layer_vec(layer, HEAD_DIM), _layer_vec(layer, V_DIM),
                  pl.BlockSpec(memory_space=pl.ANY)],
        out_specs=row_spec,
        out_shape=jax.ShapeDtypeStruct(o_init.shape, o_init.dtype),
        input_output_aliases={10: 0},
        compiler_params=_params(("parallel",)),
        name="attn_sample",
    )(q, kb, vb, cache_k, cache_v, *lam_params, o_init)


def _gelu_tanh(x):
    return 0.5 * x * (1.0 + jnp.tanh(math.sqrt(2.0 / math.pi) * (x + 0.044715 * (x * x * x))))


def _rnn_kernel(xr_ref, gate_ref, h0_ref, cbuf_ref, cw_ref, cb_ref, wa_ref, ba_ref, wx_ref, bx_ref,
                lam_ref, gn_ref, y_init_ref, y_ref, hlast_ref, ctail_ref, h_sc, win_sc, *, tb):
    del y_init_ref
    t = pl.program_id(1)

    @pl.when(t == 0)
    def _():
        h_sc[...] = h0_ref[...]
        win_sc[0:SUBLANES, :] = cbuf_ref[...]

    x = xr_ref[...]
    cw = cw_ref[...]
    win_sc[SUBLANES:SUBLANES + tb, :] = x
    xc = cb_ref[...] + x * cw[CONV_W - 1:CONV_W]
    for back in range(1, CONV_W):
        xc = xc + win_sc[SUBLANES - back:SUBLANES - back + tb, :] * cw[CONV_W - 1 - back:CONV_W - back]
    win_sc[0:SUBLANES, :] = x[tb - SUBLANES:]
    ctail_ref[...] = x[tb - SUBLANES:]

    xb = xc.astype(BF16)
    r_parts, i_parts = [], []
    for n in range(N_RNN_BLOCKS):
        sl = slice(n * LANES, (n + 1) * LANES)
        r_parts.append(jnp.dot(xb[:, sl], wa_ref[n].astype(BF16), preferred_element_type=F32))
        i_parts.append(jnp.dot(xb[:, sl], wx_ref[n].astype(BF16), preferred_element_type=F32))
    r = jax.nn.sigmoid(jnp.concatenate(r_parts, axis=-1) + ba_ref[...])
    ig = jax.nn.sigmoid(jnp.concatenate(i_parts, axis=-1) + bx_ref[...])
    neg_lam = -lam_ref[...]
    softplus = jnp.maximum(neg_lam, 0.0) + jnp.log1p(jnp.exp(-jnp.abs(neg_lam)))
    log_a = (-RG_C * r) * softplus
    a = jnp.exp(log_a)
    u = jnp.sqrt(1.0 - a * a) * (ig * xc)

    row = lax.broadcasted_iota(jnp.int32, (tb, 1), 0)
    d = 1
    while d < tb:
        keep = row >= d
        u = jnp.where(keep, a * pltpu.roll(u, d, axis=0) + u, u)
        a = jnp.where(keep, a * pltpu.roll(a, d, axis=0), a)
        d *= 2
    hs = u + a * h_sc[...]
    h_sc[...] = hs[tb - 1:tb]
    hlast_ref[...] = hs[tb - 1:tb]

    y_ref[...] = _rms(hs * _gelu_tanh(gate_ref[...]), gn_ref[...]).astype(y_ref.dtype)


def _rnn(xr, gate, h0, cbuf, weights, y_init, *, layer, state_layer, row0, n_seq, seq_len, tb):
    C = xr.shape[1]
    nb = seq_len // tb
    base = row0 // tb
    row_spec = pl.BlockSpec((tb, C), lambda b, t: (base + b * nb + t, 0))
    blk = lambda: pl.BlockSpec((None, N_RNN_BLOCKS, LANES, LANES), lambda b, t: (layer, 0, 0, 0))
    return pl.pallas_call(
        functools.partial(_rnn_kernel, tb=tb),
        grid=(n_seq, nb),
        in_specs=[row_spec, row_spec,
                  pl.BlockSpec((None, None, 1, C), lambda b, t: (state_layer, b, 0, 0)),
                  pl.BlockSpec((None, None, SUBLANES, C), lambda b, t: (state_layer, b, 0, 0)),
                  pl.BlockSpec((None, CONV_W, C), lambda b, t: (layer, 0, 0)), _layer_vec(layer, C),
                  blk(), _layer_vec(layer, C), blk(), _layer_vec(layer, C), _layer_vec(layer, C),
                  _layer_vec(layer, C),
                  pl.BlockSpec(memory_space=pl.ANY)],
        out_specs=[row_spec,
                   pl.BlockSpec((None, 1, C), lambda b, t: (b, 0, 0)),
                   pl.BlockSpec((None, SUBLANES, C), lambda b, t: (b, 0, 0))],
        out_shape=[jax.ShapeDtypeStruct(y_init.shape, y_init.dtype),
                   jax.ShapeDtypeStruct((n_seq, 1, C), F32),
                   jax.ShapeDtypeStruct((n_seq, SUBLANES, C), F32)],
        scratch_shapes=[pltpu.VMEM((1, C), F32), pltpu.VMEM((SUBLANES + tb, C), F32)],
        input_output_aliases={12: 0},
        compiler_params=_params(("arbitrary", "arbitrary")),
        name="rnn",
    )(xr, gate, h0, cbuf, *weights, y_init)


def _out_proj_kernel(o_ref, y_ref, wa_ref, wb_ref, x_ref, out_ref):
    acc = jnp.dot(o_ref[...], wa_ref[...].astype(BF16), preferred_element_type=F32)
    acc = acc + jnp.dot(y_ref[...], wb_ref[...].astype(BF16), preferred_element_type=F32)
    out_ref[...] = x_ref[...] + acc


def _out_proj(o, y, w_out, x, *, layer):
    T, D = x.shape
    half = o.shape[1]
    tm = _row_tile(T, 1408)
    tn = 512
    return pl.pallas_call(
        _out_proj_kernel,
        grid=(T // tm, D // tn),
        in_specs=[
            pl.BlockSpec((tm, half), lambda i, j: (i, 0)),
            pl.BlockSpec((tm, half), lambda i, j: (i, 0)),
            pl.BlockSpec((None, half, tn), lambda i, j: (layer, 0, j)),
            pl.BlockSpec((None, half, tn), lambda i, j: (layer, 1, j)),
            pl.BlockSpec((tm, tn), lambda i, j: (i, j)),
        ],
        out_specs=pl.BlockSpec((tm, tn), lambda i, j: (i, j)),
        out_shape=jax.ShapeDtypeStruct((T, D), F32),
        compiler_params=_params(("parallel", "arbitrary")),
        name="out_proj",
    )(o, y, w_out, w_out, x)


def _router_kernel(x_ref, g_ref, w_ref, b_ref, hn_ref, logit_ref):
    hn = _rms(x_ref[...], g_ref[...])
    hn_ref[...] = hn
    logit_ref[...] = jnp.dot(hn.astype(BF16), w_ref[...].astype(BF16), preferred_element_type=F32) + b_ref[...]


def _router(x, g, w, b, *, layer):
    T, D = x.shape
    tm = _row_tile(T, 768)
    return pl.pallas_call(
        _router_kernel,
        grid=(T // tm,),
        in_specs=[pl.BlockSpec((tm, D), lambda i: (i, 0)),
                  _layer_vec(layer, D),
                  pl.BlockSpec((None, D, ROUTER_LANES), lambda i: (layer, 0, 0)),
                  _layer_vec(layer, ROUTER_LANES)],
        out_specs=[pl.BlockSpec((tm, D), lambda i: (i, 0)),
                   pl.BlockSpec((tm, ROUTER_LANES), lambda i: (i, 0))],
        out_shape=[jax.ShapeDtypeStruct((T, D), F32), jax.ShapeDtypeStruct((T, ROUTER_LANES), F32)],
        compiler_params=_params(("parallel",)),
        name="router",
    )(x, g, w, b)


def _top1(x):
    val = jnp.max(x, axis=-1, keepdims=True)
    n = x.shape[-1]
    idx = jnp.min(jnp.where(x == val, lax.broadcasted_iota(jnp.int32, x.shape, x.ndim - 1), n), axis=-1,
                  keepdims=True)
    return val, idx


def _routing_tables(logits, n_tokens):
    T = n_tokens
    gp = jax.nn.softmax(logits[:, :N_GROUPS], axis=-1)
    g_val, g_idx = _top1(gp)
    el = logits[:, N_GROUPS:N_GROUPS + N_EXPERTS].reshape(T, N_GROUPS, EXPERTS_PER_GROUP)
    group = lax.broadcasted_iota(jnp.int32, el.shape, 1)
    el_g = jnp.sum(jnp.where(group == g_idx[:, :, None], el, 0.0), axis=1)
    v0, i0 = _top1(el_g)
    lane = lax.broadcasted_iota(jnp.int32, el_g.shape, 1)
    v1, i1 = _top1(jnp.where(lane == i0, -jnp.inf, el_g))
    e_val = jnp.concatenate([v0, v1], axis=-1)
    e_idx = jnp.concatenate([i0, i1], axis=-1)
    wts = g_val * jax.nn.softmax(e_val, axis=-1)
    eid = (g_idx * EXPERTS_PER_GROUP + e_idx).astype(jnp.int32)

    A = T * TOP_K
    B = MOE_ROWS
    n_blocks = -(-(A + N_EXPERTS * (B - 1)) // B)
    n_rows = n_blocks * B
    eid_f = eid.reshape(A)
    onehot = (eid_f[:, None] == jnp.arange(N_EXPERTS, dtype=jnp.int32)[None, :]).astype(jnp.int32)
    csum = jnp.cumsum(onehot, axis=0)
    rank = jnp.sum(onehot * csum, axis=1) - 1
    counts = csum[-1]
    padded = (counts + B - 1) // B * B
    pad_end = jnp.cumsum(padded)
    pad_start = pad_end - padded
    dest = jnp.sum(onehot * pad_start[None, :], axis=1) + rank
    row_a = jnp.full((n_rows,), -1, jnp.int32).at[dest].set(jnp.arange(A, dtype=jnp.int32))
    tok = jnp.maximum(row_a, 0) // TOP_K
    row_src = tok
    rows = jnp.arange(n_rows, dtype=jnp.int32)
    dump = TOP_K * T + ((rows // B) % 2) * B + rows % B
    row_dst = jnp.where(row_a < 0, dump, (row_a % TOP_K) * T + tok)
    blk_exp = jnp.minimum(
        jnp.sum((pad_end[None, :] <= (jnp.arange(n_blocks, dtype=jnp.int32) * B)[:, None]).astype(jnp.int32), axis=1),
        N_EXPERTS - 1).astype(jnp.int32)
    n_used = (pad_end[-1] // B).astype(jnp.int32).reshape(1)
    n_real = jnp.sum((row_a >= 0).reshape(n_blocks, B).astype(jnp.int32), axis=1)
    groups = (n_real + ROW_DMA_UNROLL - 1) // ROW_DMA_UNROLL
    return blk_exp, row_src, row_dst, n_used, groups, wts


def _moe_kernel(blk_ref, src_ref, dst_ref, nused_ref, grp_ref, hn_ref, w1_ref, w3_ref, w2_ref,
                y_ref, xg_sc, yo_sc, gsem, ssem, *, n_blocks):
    del blk_ref
    B = MOE_ROWS
    n_real = y_ref.shape[0] - 2 * B
    i = pl.program_id(0)
    n_used = nused_ref[0]
    slot = i % 2

    def for_each_group(blk, issue):
        n_groups = grp_ref[blk]
        for g in range(B // ROW_DMA_UNROLL):
            @pl.when(g < n_groups)
            def _():
                for u in range(ROW_DMA_UNROLL):
                    issue(g * ROW_DMA_UNROLL + u)

    def start_gather(blk, s):
        def issue(r):
            src = src_ref[blk * B + r]
            pltpu.make_async_copy(hn_ref.at[pl.ds(src, 1)], xg_sc.at[s, pl.ds(r, 1)],
                                  gsem.at[s]).start(priority=r % 2)
        for_each_group(blk, issue)

    def wait_gather(blk, s):
        def body(g, c):
            pltpu.make_async_copy(hn_ref.at[pl.ds(0, ROW_DMA_UNROLL)], xg_sc.at[s, pl.ds(0, ROW_DMA_UNROLL)],
                                  gsem.at[s]).wait()
            return c
        lax.fori_loop(0, grp_ref[blk], body, 0)

    def start_scatter(blk, s):
        def issue(r):
            dst = dst_ref[blk * B + r]
            pltpu.make_async_copy(yo_sc.at[s, pl.ds(r, 1)], y_ref.at[pl.ds(dst, 1)],
                                  ssem.at[s]).start(priority=r % 2)
        for_each_group(blk, issue)

    def wait_scatter(blk, s):
        def body(g, c):
            pltpu.make_async_copy(yo_sc.at[s, pl.ds(0, ROW_DMA_UNROLL)], y_ref.at[pl.ds(0, ROW_DMA_UNROLL)],
                                  ssem.at[s]).wait()
            return c
        lax.fori_loop(0, grp_ref[blk], body, 0)

    @pl.when(i == 0)
    def _():
        xg_sc[...] = jnp.zeros(xg_sc.shape, F32)
        start_gather(0, 0)
        yo_sc[0] = jnp.zeros(yo_sc.shape[1:], F32)
        for half in range(2):
            fill = pltpu.make_async_copy(yo_sc.at[0], y_ref.at[pl.ds(n_real + half * B, B)], ssem.at[0])
            fill.start()
            fill.wait()

    @pl.when(i + 1 < n_used)
    def _():
        start_gather(i + 1, 1 - slot)

    @pl.when(i < n_used)
    def _():
        wait_gather(i, slot)

        @pl.when(i >= 2)
        def _():
            wait_scatter(i - 2, slot)

        xb = xg_sc[slot].astype(BF16)
        h1 = jnp.dot(xb, w1_ref[...].astype(BF16), preferred_element_type=F32)
        h3 = jnp.dot(xb, w3_ref[...].astype(BF16), preferred_element_type=F32)
        h = (h1 * jax.nn.sigmoid(h1) * h3).astype(BF16)
        yo_sc[slot] = jnp.dot(h, w2_ref[...].astype(BF16), preferred_element_type=F32)
        start_scatter(i, slot)

    @pl.when(i == n_blocks - 1)
    def _():
        wait_scatter(n_used - 1, (n_used - 1) % 2)

        @pl.when(n_used >= 2)
        def _():
            wait_scatter(n_used - 2, n_used % 2)


def _moe(hn, blk_exp, row_src, row_dst, n_used, groups, w1, w3, w2, *, layer):
    T, D = hn.shape
    B = MOE_ROWS
    n_blocks = blk_exp.shape[0]
    de = w1.shape[3]
    grid_spec = pltpu.PrefetchScalarGridSpec(
        num_scalar_prefetch=5,
        grid=(n_blocks,),
        in_specs=[
            pl.BlockSpec(memory_space=pl.ANY),
            pl.BlockSpec((None, None, D, de), lambda i, blk, src, dst, nu, grp: (layer, blk[i], 0, 0)),
            pl.BlockSpec((None, None, D, de), lambda i, blk, src, dst, nu, grp: (layer, blk[i], 0, 0)),
            pl.BlockSpec((None, None, de, D), lambda i, blk, src, dst, nu, grp: (layer, blk[i], 0, 0)),
        ],
        out_specs=pl.BlockSpec(memory_space=pl.ANY),
        scratch_shapes=[
            pltpu.VMEM((2, B, D), F32),
            pltpu.VMEM((2, B, D), F32),
            pltpu.SemaphoreType.DMA((2,)),
            pltpu.SemaphoreType.DMA((2,)),
        ],
    )
    return pl.pallas_call(
        functools.partial(_moe_kernel, n_blocks=n_blocks),
        grid_spec=grid_spec,
        out_shape=jax.ShapeDtypeStruct((TOP_K * T + 2 * B, D), F32),
        compiler_params=_params(("arbitrary",)),
        name="moe",
    )(blk_exp, row_src, row_dst, n_used, groups, hn, w1, w3, w2)


def _ple_kernel(x_ref, y0_ref, y1_ref, wt_ref, g_ref, p_ref, wg_ref, wp_ref, out_ref, x2_sc, hn_sc, *, tn):
    j = pl.program_id(1)

    @pl.when(j == 0)
    def _():
        tm = x_ref.shape[0]
        rc = tm // COMBINE_CHUNKS
        for c0 in range(0, tm, rc):
            rows = slice(c0, c0 + rc)
            wt = wt_ref[rows, :]
            x2 = x_ref[rows, :] + (wt[:, 0:1] * y0_ref[rows, :] + wt[:, 1:2] * y1_ref[rows, :])
            hn_sc[rows, :] = _rms(x2, g_ref[...]).astype(BF16)
            for c in range(x2_sc.shape[0]):
                x2_sc[c, rows, :] = x2[:, c * tn:(c + 1) * tn]

    gate = jax.nn.sigmoid(jnp.dot(hn_sc[...], wg_ref[...].astype(BF16), preferred_element_type=F32))
    proj = jnp.dot(p_ref[...].astype(BF16), wp_ref[...].astype(BF16), preferred_element_type=F32)
    out_ref[...] = x2_sc[j] + gate * proj


def _ple(x, y_slots, wts, g, p, wg, wp, *, layer):
    T, D = x.shape
    dp = p.shape[2]
    tm = _row_tile(T, 528, 16 * COMBINE_CHUNKS)
    tn = 512
    nt = T // tm
    return pl.pallas_call(
        functools.partial(_ple_kernel, tn=tn),
        grid=(nt, D // tn),
        in_specs=[
            pl.BlockSpec((tm, D), lambda i, j: (i, 0)),
            pl.BlockSpec((tm, D), lambda i, j: (i, 0)),
            pl.BlockSpec((tm, D), lambda i, j: (i + nt, 0)),
            pl.BlockSpec((tm, TOP_K), lambda i, j: (i, 0)),
            _layer_vec(layer, D),
            pl.BlockSpec((None, tm, dp), lambda i, j: (layer, i, 0)),
            pl.BlockSpec((None, D, tn), lambda i, j: (layer, 0, j)),
            pl.BlockSpec((None, dp, tn), lambda i, j: (layer, 0, j)),
        ],
        out_specs=pl.BlockSpec((tm, tn), lambda i, j: (i, j)),
        out_shape=jax.ShapeDtypeStruct((T, D), F32),
        scratch_shapes=[pltpu.VMEM((D // tn, tm, tn), F32), pltpu.VMEM((tm, D), BF16)],
        compiler_params=_params(("parallel", "arbitrary")),
        name="ple",
    )(x, y_slots, y_slots, wts, g, p, wg, wp)


def kernel(x_prompt, x_sample, cache_k, cache_v, state_rnn_h, state_conv, p_prompt, p_sample, norm_mix, w_in, q_norm, k_norm, lambda_q1, lambda_k1, lambda_q2, lambda_k2, attn_out_norm, conv_w, conv_b, w_rg_a, b_rg_a, w_rg_x, b_rg_x, rg_lambda, rnn_out_norm, w_out, norm_ffn, w_router_group, b_router_group, w_router_expert, b_router_expert, w_exp_gate, w_exp_up, w_exp_down, norm_ple, w_ple_gate, w_ple_proj):
    bp, seq, D = x_prompt.shape
    nb, dseq, _ = x_sample.shape
    depth = w_in.shape[0]
    d_rnn = state_rnn_h.shape[-1]
    d_ple = p_prompt.shape[-1]
    assert bp == 1 and seq % CHUNK == 0 and dseq % 16 == 0 and seq % dseq == 0
    n_p = bp * seq
    n_s = nb * dseq
    T = n_p + n_s

    x = jnp.concatenate([x_prompt.reshape(n_p, D), x_sample.reshape(n_s, D)], axis=0)
    p_all = jnp.concatenate([p_prompt.reshape(depth, n_p, d_ple), p_sample.reshape(depth, n_s, d_ple)], axis=1)
    vec = lambda a: a.reshape(depth, 1, -1)
    conv_pad = jnp.pad(state_conv, ((0, 0), (0, 0), (SUBLANES - (CONV_W - 1), 0), (0, 0)))
    h0_s = state_rnn_h.reshape(depth, nb, 1, d_rnn)
    zero_h = jnp.zeros((1, bp, 1, d_rnn), F32)
    zero_conv = jnp.zeros((1, bp, SUBLANES, d_rnn), F32)
    pad_lanes = ROUTER_LANES - N_GROUPS - N_EXPERTS
    w_router = jnp.concatenate([w_router_group, w_router_expert, jnp.zeros((depth, D, pad_lanes), F32)], axis=-1)
    b_router = jnp.concatenate([b_router_group, b_router_expert, jnp.zeros((depth, pad_lanes), F32)], axis=-1)
    tb_p = _row_tile(seq, 256, SUBLANES)

    norm_mix_v, qg_v, kg_v = vec(norm_mix), vec(q_norm), vec(k_norm)
    lam_params = (vec(lambda_q1), vec(lambda_k1), vec(lambda_q2), vec(lambda_k2), vec(attn_out_norm))
    rnn_w = (conv_w, vec(conv_b), w_rg_a, vec(b_rg_a), w_rg_x, vec(b_rg_x), vec(rg_lambda), vec(rnn_out_norm))
    norm_ffn_v, b_router_v, norm_ple_v = vec(norm_ffn), vec(b_router), vec(norm_ple)

    ks, vs = [], []
    hp, cp, hs_, cs_ = [], [], [], []
    for l in range(depth):
        lam0 = _lambda_init(l)
        q, k, v, xr, gate, kb, vb = _in_proj(x, norm_mix_v, w_in, qg_v, kg_v, layer=l)
        ks.append(k)
        vs.append(v)
        o = jnp.zeros((T, D_ATTN), BF16)
        o = _attn_prompt(q, kb, vb, lam_params, qg_v, kg_v, o, layer=l, n_prompt=n_p, lam0=lam0)
        o = _attn_sample(q, kb, vb, cache_k, cache_v, lam_params, o, layer=l, n_prompt=n_p, rows=dseq, lam0=lam0)

        y = jnp.zeros((T, d_rnn), BF16)
        y, h_p, c_p = _rnn(xr, gate, zero_h, zero_conv, rnn_w, y, layer=l, state_layer=0,
                           row0=0, n_seq=bp, seq_len=seq, tb=tb_p)
        y, h_s, c_s = _rnn(xr, gate, h0_s, conv_pad, rnn_w, y, layer=l, state_layer=l,
                           row0=n_p, n_seq=nb, seq_len=dseq, tb=dseq)

        x = _out_proj(o, y, w_out, x, layer=l)

        hn, logits = _router(x, norm_ffn_v, w_router, b_router_v, layer=l)
        blk_exp, row_src, row_dst, n_used, groups, wts = _routing_tables(logits, T)
        y_slots = _moe(hn, blk_exp, row_src, row_dst, n_used, groups, w_exp_gate, w_exp_up, w_exp_down, layer=l)

        x = _ple(x, y_slots, wts, norm_ple_v, p_all, w_ple_gate, w_ple_proj, layer=l)

        hp.append(h_p[:, 0])
        cp.append(c_p[:, SUBLANES - (CONV_W - 1):])
        hs_.append(h_s[:, 0])
        cs_.append(c_s[:, SUBLANES - (CONV_W - 1):])

    k_p, k_s, v_p, v_s = _kv_layout(ks, vs, n_prompt=n_p, n_sample=n_s)
    return (x[:n_p].reshape(bp, seq, D), x[n_p:].reshape(nb, dseq, D),
            k_p.reshape(depth, bp, seq, N_HEADS, QK_DIM), v_p.reshape(depth, bp, seq, N_HEADS, V_DIM),
            jnp.stack(hp), jnp.stack(cp),
            k_s.reshape(depth, nb, dseq, N_HEADS, QK_DIM), v_s.reshape(depth, nb, dseq, N_HEADS, V_DIM),
            jnp.stack(hs_), jnp.stack(cs_))
```

```python
import functools
import math

import jax
import jax.numpy as jnp
from jax import lax
from jax.experimental import pallas as pl
from jax.experimental.pallas import tpu as pltpu

F32 = jnp.float32
BF16 = jnp.bfloat16

CHUNK = 64
N_HEADS = 8
HEAD_DIM = 64
QK_DIM = 2 * HEAD_DIM
V_DIM = 2 * HEAD_DIM
D_ATTN = N_HEADS * V_DIM
N_RNN_BLOCKS = 8
CONV_W = 4
RG_C = 8.0
N_GROUPS = 4
EXPERTS_PER_GROUP = 8
N_EXPERTS = N_GROUPS * EXPERTS_PER_GROUP
TOP_K = 2
EPS = 1e-6
NEG_INF = -1e30

LANES = 128
SUBLANES = 8
VMEM_LIMIT_BYTES = 56 * 1024 * 1024
MOE_ROWS = 256
ROUTER_LANES = 128
ROW_DMA_UNROLL = 8
COMBINE_CHUNKS = 3


def _lambda_init(layer):
    return 0.8 - 0.6 * math.exp(-0.3 * layer)


def _row_tile(total, target, multiple=16):
    best = None
    for t in range(multiple, min(total, target) + 1, multiple):
        if total % t == 0:
            best = t
    assert best is not None, (total, target)
    return best


def _params(sem):
    return pltpu.CompilerParams(dimension_semantics=sem, vmem_limit_bytes=VMEM_LIMIT_BYTES)


def _layer_vec(l, n):
    return pl.BlockSpec((None, 1, n), lambda *_: (l, 0, 0))


def _rms(x, gain):
    ms = jnp.mean(x * x, axis=-1, keepdims=True)
    return x * lax.rsqrt(ms + EPS) * gain


def _half_norm(a, gain, lo):
    sq = a * a
    s_lo = jnp.sum(jnp.where(lo, sq, 0.0), axis=-1, keepdims=True)
    s_hi = jnp.sum(jnp.where(lo, 0.0, sq), axis=-1, keepdims=True)
    ms = jnp.where(lo, s_lo, s_hi) * (1.0 / HEAD_DIM)
    return a * lax.rsqrt(ms + EPS) * gain


def _in_proj_kernel(x_ref, g_ref, w_ref, qg_ref, kg_ref,
                    q_ref, k_ref, v_ref, xr_ref, gate_ref, kb_ref, vb_ref, hn_ref, *, steps_per_section, tn):
    j = pl.program_id(1)

    @pl.when(j == 0)
    def _():
        hn_ref[...] = _rms(x_ref[...], g_ref[...]).astype(BF16)

    acc = jnp.dot(hn_ref[...], w_ref[...], preferred_element_type=F32)
    sec = j // steps_per_section
    lo = lax.broadcasted_iota(jnp.int32, (1, LANES), 1) < HEAD_DIM

    @pl.when(sec == 0)
    def _():
        for h in range(tn // LANES):
            sl = slice(h * LANES, (h + 1) * LANES)
            q_ref[:, sl] = (_half_norm(acc[:, sl], qg_ref[...], lo) * (HEAD_DIM ** -0.5)).astype(BF16)

    @pl.when(sec == 1)
    def _():
        for h in range(tn // LANES):
            sl = slice(h * LANES, (h + 1) * LANES)
            kn = _half_norm(acc[:, sl], kg_ref[...], lo)
            k_ref[:, sl] = kn
            kb_ref[:, sl] = kn.astype(BF16)

    @pl.when(sec == 2)
    def _():
        v_ref[...] = acc
        vb_ref[...] = acc.astype(BF16)

    @pl.when(sec == 3)
    def _():
        xr_ref[...] = acc

    @pl.when(sec == 4)
    def _():
        gate_ref[...] = acc


def _in_proj(x, g, w_in, qg, kg, *, layer):
    T, D = x.shape
    tm = _row_tile(T, 1056)
    tn = 512
    sec_w = D_ATTN
    sps = sec_w // tn
    n_steps = w_in.shape[2] // tn
    col = lambda sec: (lambda j: jnp.clip(j - sec * sps, 0, sps - 1))

    def out_spec(sec):
        return pl.BlockSpec((tm, tn), lambda i, j: (i, col(sec)(j)))

    flat = jax.ShapeDtypeStruct((T, sec_w), F32)
    flat_b = jax.ShapeDtypeStruct((T, sec_w), BF16)
    return pl.pallas_call(
        functools.partial(_in_proj_kernel, steps_per_section=sps, tn=tn),
        grid=(T // tm, n_steps),
        in_specs=[
            pl.BlockSpec((tm, D), lambda i, j: (i, 0)),
            _layer_vec(layer, D),
            pl.BlockSpec((None, D, tn), lambda i, j: (layer, 0, j)),
            _layer_vec(layer, LANES),
            _layer_vec(layer, LANES),
        ],
        out_specs=[out_spec(s) for s in (0, 1, 2, 3, 4, 1, 2)],
        out_shape=[flat_b, flat, flat, flat, flat, flat_b, flat_b],
        scratch_shapes=[pltpu.VMEM((tm, D), BF16)],
        compiler_params=_params(("parallel", "arbitrary")),
        name="in_proj",
    )(x, g, w_in, qg, kg)


def _kv_layout_kernel(*refs, depth, prompt_tiles):
    srcs = refs[:2 * depth]
    kp_ref, ks_ref, vp_ref, vs_ref = refs[2 * depth:]
    l = pl.program_id(0)
    i = pl.program_id(1)

    def spread(src, dst):
        for h in range(N_HEADS):
            dst[:, h, :] = src[:, h * LANES:(h + 1) * LANES]

    for d in range(depth):
        @pl.when(jnp.logical_and(l == d, i < prompt_tiles))
        def _():
            spread(srcs[d], kp_ref)
            spread(srcs[depth + d], vp_ref)

        @pl.when(jnp.logical_and(l == d, i >= prompt_tiles))
        def _():
            spread(srcs[d], ks_ref)
            spread(srcs[depth + d], vs_ref)


def _kv_layout(ks, vs, *, n_prompt, n_sample):
    depth = len(ks)
    T, width = ks[0].shape
    tr = math.gcd(n_prompt, n_sample)
    nt, np_t, ns_t = T // tr, n_prompt // tr, n_sample // tr

    def src_spec(d):
        return pl.BlockSpec((tr, width), lambda l, i: (jnp.where(l < d, 0, jnp.where(l > d, nt - 1, i)), 0))

    blk = (None, tr, N_HEADS, LANES)
    prompt_spec = pl.BlockSpec(blk, lambda l, i: (l, jnp.minimum(i, np_t - 1), 0, 0))
    sample_spec = pl.BlockSpec(blk, lambda l, i: (l, jnp.clip(i - np_t, 0, ns_t - 1), 0, 0))
    shape = lambda n: jax.ShapeDtypeStruct((depth, n, N_HEADS, LANES), F32)
    return pl.pallas_call(
        functools.partial(_kv_layout_kernel, depth=depth, prompt_tiles=np_t),
        grid=(depth, nt),
        in_specs=[src_spec(d) for d in range(depth)] * 2,
        out_specs=[prompt_spec, sample_spec, prompt_spec, sample_spec],
        out_shape=[shape(n_prompt), shape(n_sample), shape(n_prompt), shape(n_sample)],
        compiler_params=_params(("arbitrary", "arbitrary")),
        name="kv_layout",
    )(*ks, *vs)


def _lambda_value(lq1_ref, lk1_ref, lq2_ref, lk2_ref, lam0):
    s1 = jnp.sum(lq1_ref[...] * lk1_ref[...], axis=-1, keepdims=True)
    s2 = jnp.sum(lq2_ref[...] * lk2_ref[...], axis=-1, keepdims=True)
    return jnp.exp(s1) - jnp.exp(s2) + lam0


def _split_q(q):
    lo = lax.broadcasted_iota(jnp.int32, q.shape, 1) < HEAD_DIM
    zero = jnp.zeros_like(q)
    return jnp.concatenate([jnp.where(lo, q, zero), jnp.where(lo, zero, q)], axis=0)


def _scores(q2, kb):
    return lax.dot_general(q2, kb, (((1,), (1,)), ((), ())), preferred_element_type=F32)


def _finish_heads(acc, l, lam, gout, rows, lam0):
    o = acc[:rows] / l[:rows] - lam * (acc[rows:] / l[rows:])
    return _rms(o, gout) * (1.0 - lam0)


def _attn_prompt_kernel(q_ref, k_ref, v_ref, lq1_ref, lk1_ref, lq2_ref, lk2_ref, gout_ref, o_init_ref,
                        o_ref, q2_sc, m_sc, acc_sc, *, bq, lam0):
    del o_init_ref
    i = pl.program_id(1)
    bk = bq
    q2_sc[...] = _split_q(q_ref[...])
    m_sc[...] = jnp.full(m_sc.shape, NEG_INF, F32)
    acc_sc[...] = jnp.zeros(acc_sc.shape, F32)
    ones = jnp.ones((bk, LANES), BF16)

    def update(off, masked):
        kb = k_ref[pl.ds(off, bk), :]
        vb = v_ref[pl.ds(off, bk), :]
        s = _scores(q2_sc[...], kb)
        if masked:
            q_chunk = (lax.broadcasted_iota(jnp.int32, s.shape, 0) % bq) // CHUNK
            k_chunk = lax.broadcasted_iota(jnp.int32, s.shape, 1) // CHUNK
            s = jnp.where(k_chunk <= q_chunk, s, NEG_INF)
        m_old = m_sc[...]
        m_new = jnp.maximum(m_old, jnp.max(s, axis=-1, keepdims=True))
        pr = jnp.exp(s - jnp.concatenate([m_new] * (bk // LANES), axis=1))
        alpha = jnp.exp(m_old - m_new)
        pv = jnp.dot(pr.astype(BF16), jnp.concatenate([vb, ones], axis=1), preferred_element_type=F32)
        acc_sc[...] = jnp.concatenate([alpha, alpha], axis=1) * acc_sc[...] + pv
        m_sc[...] = m_new

    def body(j, carry):
        update(pl.multiple_of(j * bk, bk), False)
        return carry

    lax.fori_loop(0, i, body, 0)
    update(pl.multiple_of(i * bk, bk), True)
    lam = _lambda_value(lq1_ref, lk1_ref, lq2_ref, lk2_ref, lam0)
    acc = acc_sc[...]
    o_ref[...] = _finish_heads(acc[:, :LANES], acc[:, LANES:], lam, gout_ref[...], bq, lam0).astype(o_ref.dtype)


def _score_bound(qg_ref, kg_ref):
    gq = jnp.max(jnp.abs(qg_ref[...]), axis=-1, keepdims=True)
    gk = jnp.max(jnp.abs(kg_ref[...]), axis=-1, keepdims=True)
    return (HEAD_DIM ** 0.5) * gq * gk


def _attn_prompt_shift_kernel(q_ref, k_ref, v_ref, lq1_ref, lk1_ref, lq2_ref, lk2_ref, gout_ref, qg_ref, kg_ref,
                              o_init_ref, o_ref, q2_sc, acc_sc, *, bq, lam0):
    del o_init_ref
    i = pl.program_id(1)
    bk = bq
    hb = bq // 2
    for a in range(2):
        q2_sc[a * bq:(a + 1) * bq, :] = _split_q(q_ref[a * hb:(a + 1) * hb, :])
    acc_sc[...] = jnp.zeros(acc_sc.shape, F32)
    shift = _score_bound(qg_ref, kg_ref)

    def accumulate(rows, off, n_keys, mask_from_half):
        kb = k_ref[pl.ds(off, n_keys), :]
        vb = v_ref[pl.ds(off, n_keys), :]
        s = _scores(q2_sc[rows, :], kb) - shift
        if mask_from_half is not None:
            q_chunk = (lax.broadcasted_iota(jnp.int32, s.shape, 0) % hb + mask_from_half * hb) // CHUNK
            k_chunk = lax.broadcasted_iota(jnp.int32, s.shape, 1) // CHUNK
            s = jnp.where(k_chunk <= q_chunk, s, NEG_INF)
        pr = jnp.exp(s).astype(BF16)
        ones = jnp.ones((n_keys, LANES), BF16)
        acc_sc[rows, :] += jnp.dot(pr, jnp.concatenate([vb, ones], axis=1), preferred_element_type=F32)

    everything = slice(0, 2 * bq)

    def body(jp, carry):
        accumulate(everything, pl.multiple_of(2 * jp * bk, bk), bk, None)
        accumulate(everything, pl.multiple_of((2 * jp + 1) * bk, bk), bk, None)
        return carry

    lax.fori_loop(0, i // 2, body, 0)

    @pl.when(i % 2 == 1)
    def _():
        accumulate(everything, pl.multiple_of((i - 1) * bk, bk), bk, None)

    diag = pl.multiple_of(i * bk, bk)
    accumulate(slice(0, bq), diag, hb, 0)
    accumulate(slice(bq, 2 * bq), diag, bk, 1)
    lam = _lambda_value(lq1_ref, lk1_ref, lq2_ref, lk2_ref, lam0)
    for a in range(2):
        acc = acc_sc[a * bq:(a + 1) * bq, :]
        o_ref[a * hb:(a + 1) * hb, :] = _finish_heads(
            acc[:, :LANES], acc[:, LANES:], lam, gout_ref[...], hb, lam0).astype(o_ref.dtype)


MAX_FIXED_SHIFT = 30.0


def _attn_prompt(q, kb, vb, lam_params, qg, kg, o_init, *, layer, n_prompt, lam0):
    bq = _row_tile(n_prompt, 1024, 2 * LANES)
    common_in = [
        pl.BlockSpec((bq, LANES), lambda h, i: (i, h)),
        pl.BlockSpec((n_prompt, LANES), lambda h, i: (0, h)),
        pl.BlockSpec((n_prompt, LANES), lambda h, i: (0, h)),
        _layer_vec(layer, HEAD_DIM), _layer_vec(layer, HEAD_DIM), _layer_vec(layer, HEAD_DIM),
        _layer_vec(layer, HEAD_DIM), _layer_vec(layer, V_DIM),
    ]
    common = dict(
        grid=(N_HEADS, n_prompt // bq),
        out_specs=pl.BlockSpec((bq, LANES), lambda h, i: (i, h)),
        out_shape=jax.ShapeDtypeStruct(o_init.shape, o_init.dtype),
        compiler_params=_params(("parallel", "arbitrary")),
    )
    q2_scratch = pltpu.VMEM((2 * bq, LANES), BF16)
    acc_scratch = pltpu.VMEM((2 * bq, 2 * LANES), F32)

    def streaming(o):
        return pl.pallas_call(
            functools.partial(_attn_prompt_kernel, bq=bq, lam0=lam0),
            in_specs=common_in + [pl.BlockSpec(memory_space=pl.ANY)],
            scratch_shapes=[q2_scratch, pltpu.VMEM((2 * bq, LANES), F32), acc_scratch],
            input_output_aliases={8: 0},
            name="attn_prompt",
            **common,
        )(q, kb, vb, *lam_params, o)

    def fixed_shift(o):
        return pl.pallas_call(
            functools.partial(_attn_prompt_shift_kernel, bq=bq, lam0=lam0),
            in_specs=common_in + [_layer_vec(layer, LANES), _layer_vec(layer, LANES),
                                  pl.BlockSpec(memory_space=pl.ANY)],
            scratch_shapes=[q2_scratch, acc_scratch],
            input_output_aliases={10: 0},
            name="attn_prompt_shift",
            **common,
        )(q, kb, vb, *lam_params, qg, kg, o)

    bound = (HEAD_DIM ** 0.5) * jnp.max(jnp.abs(qg[layer])) * jnp.max(jnp.abs(kg[layer]))
    return lax.cond(bound <= MAX_FIXED_SHIFT, fixed_shift, streaming, o_init)


def _attn_sample_kernel(q_ref, k_ref, v_ref, ck_ref, cv_ref, lq1_ref, lk1_ref, lq2_ref, lk2_ref, gout_ref,
                        o_init_ref, o_ref, *, rows, lam0):
    del o_init_ref
    lam = _lambda_value(lq1_ref, lk1_ref, lq2_ref, lk2_ref, lam0)
    for h in range(N_HEADS):
        sl = slice(h * LANES, (h + 1) * LANES)
        q2 = _split_q(q_ref[:, sl])
        s_c = _scores(q2, ck_ref[:, h, :].astype(BF16))
        s_n = _scores(q2, k_ref[:, sl])
        m = jnp.maximum(jnp.max(s_c, axis=-1, keepdims=True), jnp.max(s_n, axis=-1, keepdims=True))
        p_c = jnp.exp(s_c - m)
        p_n = jnp.exp(s_n - m)
        l = jnp.sum(p_c, axis=-1, keepdims=True) + jnp.sum(p_n, axis=-1, keepdims=True)
        acc = (jnp.dot(p_c.astype(BF16), cv_ref[:, h, :].astype(BF16), preferred_element_type=F32)
               + jnp.dot(p_n.astype(BF16), v_ref[:, sl], preferred_element_type=F32))
        o_ref[:, sl] = _finish_heads(acc, l, lam, gout_ref[...], rows, lam0).astype(o_ref.dtype)


def _attn_sample(q, kb, vb, cache_k, cache_v, lam_params, o_init, *, layer, n_prompt, rows, lam0):
    n_streams, past = cache_k.shape[1], cache_k.shape[2]
    base = n_prompt // rows
    row_spec = pl.BlockSpec((rows, D_ATTN), lambda b: (base + b, 0))
    cache_spec = pl.BlockSpec((None, None, past, N_HEADS, LANES), lambda b: (layer, b, 0, 0, 0))
    return pl.pallas_call(
        functools.partial(_attn_sample_kernel, rows=rows, lam0=lam0),
        grid=(n_streams,),
        in_specs=[row_spec, row_spec, row_spec, cache_spec, cache_spec,
                  _layer_vec(layer, HEAD_DIM), _layer_vec(layer, HEAD_DIM), _layer_vec(layer, HEAD_DIM),
                  _layer_vec(layer, HEAD_DIM), _layer_vec(layer, V_DIM),
                  pl.BlockSpec(memory_space=pl.ANY)],
        out_specs=row_spec,
        out_shape=jax.ShapeDtypeStruct(o_init.shape, o_init.dtype),
        input_output_aliases={10: 0},
        compiler_params=_params(("parallel",)),
        name="attn_sample",
    )(q, kb, vb, cache_k, cache_v, *lam_params, o_init)


def _gelu_tanh(x):
    return 0.5 * x * (1.0 + jnp.tanh(math.sqrt(2.0 / math.pi) * (x + 0.044715 * (x * x * x))))


def _rnn_kernel(xr_ref, gate_ref, h0_ref, cbuf_ref, cw_ref, cb_ref, wa_ref, ba_ref, wx_ref, bx_ref,
                lam_ref, gn_ref, y_init_ref, y_ref, hlast_ref, ctail_ref, h_sc, win_sc, *, tb):
    del y_init_ref
    t = pl.program_id(1)

    @pl.when(t == 0)
    def _():
        h_sc[...] = h0_ref[...]
        win_sc[0:SUBLANES, :] = cbuf_ref[...]

    x = xr_ref[...]
    cw = cw_ref[...]
    win_sc[SUBLANES:SUBLANES + tb, :] = x
    xc = cb_ref[...] + x * cw[CONV_W - 1:CONV_W]
    for back in range(1, CONV_W):
        xc = xc + win_sc[SUBLANES - back:SUBLANES - back + tb, :] * cw[CONV_W - 1 - back:CONV_W - back]
    win_sc[0:SUBLANES, :] = x[tb - SUBLANES:]
    ctail_ref[...] = x[tb - SUBLANES:]

    xb = xc.astype(BF16)
    r_parts, i_parts = [], []
    for n in range(N_RNN_BLOCKS):
        sl = slice(n * LANES, (n + 1) * LANES)
        r_parts.append(jnp.dot(xb[:, sl], wa_ref[n].astype(BF16), preferred_element_type=F32))
        i_parts.append(jnp.dot(xb[:, sl], wx_ref[n].astype(BF16), preferred_element_type=F32))
    r = jax.nn.sigmoid(jnp.concatenate(r_parts, axis=-1) + ba_ref[...])
    ig = jax.nn.sigmoid(jnp.concatenate(i_parts, axis=-1) + bx_ref[...])
    neg_lam = -lam_ref[...]
    softplus = jnp.maximum(neg_lam, 0.0) + jnp.log1p(jnp.exp(-jnp.abs(neg_lam)))
    log_a = (-RG_C * r) * softplus
    a = jnp.exp(log_a)
    u = jnp.sqrt(1.0 - a * a) * (ig * xc)

    row = lax.broadcasted_iota(jnp.int32, (tb, 1), 0)
    d = 1
    while d < tb:
        keep = row >= d
        u = jnp.where(keep, a * pltpu.roll(u, d, axis=0) + u, u)
        a = jnp.where(keep, a * pltpu.roll(a, d, axis=0), a)
        d *= 2
    hs = u + a * h_sc[...]
    h_sc[...] = hs[tb - 1:tb]
    hlast_ref[...] = hs[tb - 1:tb]

    y_ref[...] = _rms(hs * _gelu_tanh(gate_ref[...]), gn_ref[...]).astype(y_ref.dtype)


def _rnn(xr, gate, h0, cbuf, weights, y_init, *, layer, state_layer, row0, n_seq, seq_len, tb):
    C = xr.shape[1]
    nb = seq_len // tb
    base = row0 // tb
    row_spec = pl.BlockSpec((tb, C), lambda b, t: (base + b * nb + t, 0))
    blk = lambda: pl.BlockSpec((None, N_RNN_BLOCKS, LANES, LANES), lambda b, t: (layer, 0, 0, 0))
    return pl.pallas_call(
        functools.partial(_rnn_kernel, tb=tb),
        grid=(n_seq, nb),
        in_specs=[row_spec, row_spec,
                  pl.BlockSpec((None, None, 1, C), lambda b, t: (state_layer, b, 0, 0)),
                  pl.BlockSpec((None, None, SUBLANES, C), lambda b, t: (state_layer, b, 0, 0)),
                  pl.BlockSpec((None, CONV_W, C), lambda b, t: (layer, 0, 0)), _layer_vec(layer, C),
                  blk(), _layer_vec(layer, C), blk(), _layer_vec(layer, C), _layer_vec(layer, C),
                  _layer_vec(layer, C),
                  pl.BlockSpec(memory_space=pl.ANY)],
        out_specs=[row_spec,
                   pl.BlockSpec((None, 1, C), lambda b, t: (b, 0, 0)),
                   pl.BlockSpec((None, SUBLANES, C), lambda b, t: (b, 0, 0))],
        out_shape=[jax.ShapeDtypeStruct(y_init.shape, y_init.dtype),
                   jax.ShapeDtypeStruct((n_seq, 1, C), F32),
                   jax.ShapeDtypeStruct((n_seq, SUBLANES, C), F32)],
        scratch_shapes=[pltpu.VMEM((1, C), F32), pltpu.VMEM((SUBLANES + tb, C), F32)],
        input_output_aliases={12: 0},
        compiler_params=_params(("arbitrary", "arbitrary")),
        name="rnn",
    )(xr, gate, h0, cbuf, *weights, y_init)


def _out_proj_kernel(o_ref, y_ref, wa_ref, wb_ref, x_ref, out_ref):
    acc = jnp.dot(o_ref[...], wa_ref[...], preferred_element_type=F32)
    acc = acc + jnp.dot(y_ref[...], wb_ref[...], preferred_element_type=F32)
    out_ref[...] = x_ref[...] + acc


def _out_proj(o, y, w_out, x, *, layer):
    T, D = x.shape
    half = o.shape[1]
    tm = _row_tile(T, 1408)
    tn = 512
    return pl.pallas_call(
        _out_proj_kernel,
        grid=(T // tm, D // tn),
        in_specs=[
            pl.BlockSpec((tm, half), lambda i, j: (i, 0)),
            pl.BlockSpec((tm, half), lambda i, j: (i, 0)),
            pl.BlockSpec((None, half, tn), lambda i, j: (layer, 0, j)),
            pl.BlockSpec((None, half, tn), lambda i, j: (layer, 1, j)),
            pl.BlockSpec((tm, tn), lambda i, j: (i, j)),
        ],
        out_specs=pl.BlockSpec((tm, tn), lambda i, j: (i, j)),
        out_shape=jax.ShapeDtypeStruct((T, D), F32),
        compiler_params=_params(("parallel", "arbitrary")),
        name="out_proj",
    )(o, y, w_out, w_out, x)


def _router_kernel(x_ref, g_ref, w_ref, b_ref, hn_ref, logit_ref):
    hn = _rms(x_ref[...], g_ref[...])
    hn_ref[...] = hn
    logit_ref[...] = jnp.dot(hn.astype(BF16), w_ref[...].astype(BF16), preferred_element_type=F32) + b_ref[...]


def _router(x, g, w, b, *, layer):
    T, D = x.shape
    tm = _row_tile(T, 768)
    return pl.pallas_call(
        _router_kernel,
        grid=(T // tm,),
        in_specs=[pl.BlockSpec((tm, D), lambda i: (i, 0)),
                  _layer_vec(layer, D),
                  pl.BlockSpec((None, D, ROUTER_LANES), lambda i: (layer, 0, 0)),
                  _layer_vec(layer, ROUTER_LANES)],
        out_specs=[pl.BlockSpec((tm, D), lambda i: (i, 0)),
                   pl.BlockSpec((tm, ROUTER_LANES), lambda i: (i, 0))],
        out_shape=[jax.ShapeDtypeStruct((T, D), F32), jax.ShapeDtypeStruct((T, ROUTER_LANES), F32)],
        compiler_params=_params(("parallel",)),
        name="router",
    )(x, g, w, b)


def _top1(x):
    val = jnp.max(x, axis=-1, keepdims=True)
    n = x.shape[-1]
    idx = jnp.min(jnp.where(x == val, lax.broadcasted_iota(jnp.int32, x.shape, x.ndim - 1), n), axis=-1,
                  keepdims=True)
    return val, idx


def _routing_tables(logits, n_tokens):
    T = n_tokens
    gp = jax.nn.softmax(logits[:, :N_GROUPS], axis=-1)
    g_val, g_idx = _top1(gp)
    el = logits[:, N_GROUPS:N_GROUPS + N_EXPERTS].reshape(T, N_GROUPS, EXPERTS_PER_GROUP)
    group = lax.broadcasted_iota(jnp.int32, el.shape, 1)
    el_g = jnp.sum(jnp.where(group == g_idx[:, :, None], el, 0.0), axis=1)
    v0, i0 = _top1(el_g)
    lane = lax.broadcasted_iota(jnp.int32, el_g.shape, 1)
    v1, i1 = _top1(jnp.where(lane == i0, -jnp.inf, el_g))
    e_val = jnp.concatenate([v0, v1], axis=-1)
    e_idx = jnp.concatenate([i0, i1], axis=-1)
    wts = g_val * jax.nn.softmax(e_val, axis=-1)
    eid = (g_idx * EXPERTS_PER_GROUP + e_idx).astype(jnp.int32)

    A = T * TOP_K
    B = MOE_ROWS
    n_blocks = -(-(A + N_EXPERTS * (B - 1)) // B)
    n_rows = n_blocks * B
    eid_f = eid.reshape(A)
    onehot = (eid_f[:, None] == jnp.arange(N_EXPERTS, dtype=jnp.int32)[None, :]).astype(jnp.int32)
    csum = jnp.cumsum(onehot, axis=0)
    rank = jnp.sum(onehot * csum, axis=1) - 1
    counts = csum[-1]
    padded = (counts + B - 1) // B * B
    pad_end = jnp.cumsum(padded)
    pad_start = pad_end - padded
    dest = jnp.sum(onehot * pad_start[None, :], axis=1) + rank
    row_a = jnp.full((n_rows,), -1, jnp.int32).at[dest].set(jnp.arange(A, dtype=jnp.int32))
    tok = jnp.maximum(row_a, 0) // TOP_K
    row_src = tok
    rows = jnp.arange(n_rows, dtype=jnp.int32)
    dump = TOP_K * T + ((rows // B) % 2) * B + rows % B
    row_dst = jnp.where(row_a < 0, dump, (row_a % TOP_K) * T + tok)
    blk_exp = jnp.minimum(
        jnp.sum((pad_end[None, :] <= (jnp.arange(n_blocks, dtype=jnp.int32) * B)[:, None]).astype(jnp.int32), axis=1),
        N_EXPERTS - 1).astype(jnp.int32)
    n_used = (pad_end[-1] // B).astype(jnp.int32).reshape(1)
    n_real = jnp.sum((row_a >= 0).reshape(n_blocks, B).astype(jnp.int32), axis=1)
    groups = (n_real + ROW_DMA_UNROLL - 1) // ROW_DMA_UNROLL
    return blk_exp, row_src, row_dst, n_used, groups, wts


def _moe_kernel(blk_ref, src_ref, dst_ref, nused_ref, grp_ref, hn_ref, w1_ref, w3_ref, w2_ref,
                y_ref, xg_sc, yo_sc, gsem, ssem, *, n_blocks):
    del blk_ref
    B = MOE_ROWS
    n_real = y_ref.shape[0] - 2 * B
    i = pl.program_id(0)
    n_used = nused_ref[0]
    slot = i % 2

    def for_each_group(blk, issue):
        n_groups = grp_ref[blk]
        for g in range(B // ROW_DMA_UNROLL):
            @pl.when(g < n_groups)
            def _():
                for u in range(ROW_DMA_UNROLL):
                    issue(g * ROW_DMA_UNROLL + u)

    def start_gather(blk, s):
        def issue(r):
            src = src_ref[blk * B + r]
            pltpu.make_async_copy(hn_ref.at[pl.ds(src, 1)], xg_sc.at[s, pl.ds(r, 1)],
                                  gsem.at[s]).start(priority=r % 2)
        for_each_group(blk, issue)

    def wait_gather(blk, s):
        def body(g, c):
            pltpu.make_async_copy(hn_ref.at[pl.ds(0, ROW_DMA_UNROLL)], xg_sc.at[s, pl.ds(0, ROW_DMA_UNROLL)],
                                  gsem.at[s]).wait()
            return c
        lax.fori_loop(0, grp_ref[blk], body, 0)

    def start_scatter(blk, s):
        def issue(r):
            dst = dst_ref[blk * B + r]
            pltpu.make_async_copy(yo_sc.at[s, pl.ds(r, 1)], y_ref.at[pl.ds(dst, 1)],
                                  ssem.at[s]).start(priority=r % 2)
        for_each_group(blk, issue)

    def wait_scatter(blk, s):
        def body(g, c):
            pltpu.make_async_copy(yo_sc.at[s, pl.ds(0, ROW_DMA_UNROLL)], y_ref.at[pl.ds(0, ROW_DMA_UNROLL)],
                                  ssem.at[s]).wait()
            return c
        lax.fori_loop(0, grp_ref[blk], body, 0)

    @pl.when(i == 0)
    def _():
        xg_sc[...] = jnp.zeros(xg_sc.shape, F32)
        start_gather(0, 0)
        yo_sc[0] = jnp.zeros(yo_sc.shape[1:], F32)
        for half in range(2):
            fill = pltpu.make_async_copy(yo_sc.at[0], y_ref.at[pl.ds(n_real + half * B, B)], ssem.at[0])
            fill.start()
            fill.wait()

    @pl.when(i + 1 < n_used)
    def _():
        start_gather(i + 1, 1 - slot)

    @pl.when(i < n_used)
    def _():
        wait_gather(i, slot)

        @pl.when(i >= 2)
        def _():
            wait_scatter(i - 2, slot)

        xb = xg_sc[slot].astype(BF16)
        h1 = jnp.dot(xb, w1_ref[...].astype(BF16), preferred_element_type=F32)
        h3 = jnp.dot(xb, w3_ref[...].astype(BF16), preferred_element_type=F32)
        h = (h1 * jax.nn.sigmoid(h1) * h3).astype(BF16)
        yo_sc[slot] = jnp.dot(h, w2_ref[...].astype(BF16), preferred_element_type=F32)
        start_scatter(i, slot)

    @pl.when(i == n_blocks - 1)
    def _():
        wait_scatter(n_used - 1, (n_used - 1) % 2)

        @pl.when(n_used >= 2)
        def _():
            wait_scatter(n_used - 2, n_used % 2)


def _moe(hn, blk_exp, row_src, row_dst, n_used, groups, w1, w3, w2, *, layer):
    T, D = hn.shape
    B = MOE_ROWS
    n_blocks = blk_exp.shape[0]
    de = w1.shape[3]
    grid_spec = pltpu.PrefetchScalarGridSpec(
        num_scalar_prefetch=5,
        grid=(n_blocks,),
        in_specs=[
            pl.BlockSpec(memory_space=pl.ANY),
            pl.BlockSpec((None, None, D, de), lambda i, blk, src, dst, nu, grp: (layer, blk[i], 0, 0)),
            pl.BlockSpec((None, None, D, de), lambda i, blk, src, dst, nu, grp: (layer, blk[i], 0, 0)),
            pl.BlockSpec((None, None, de, D), lambda i, blk, src, dst, nu, grp: (layer, blk[i], 0, 0)),
        ],
        out_specs=pl.BlockSpec(memory_space=pl.ANY),
        scratch_shapes=[
            pltpu.VMEM((2, B, D), F32),
            pltpu.VMEM((2, B, D), F32),
            pltpu.SemaphoreType.DMA((2,)),
            pltpu.SemaphoreType.DMA((2,)),
        ],
    )
    return pl.pallas_call(
        functools.partial(_moe_kernel, n_blocks=n_blocks),
        grid_spec=grid_spec,
        out_shape=jax.ShapeDtypeStruct((TOP_K * T + 2 * B, D), F32),
        compiler_params=_params(("arbitrary",)),
        name="moe",
    )(blk_exp, row_src, row_dst, n_used, groups, hn, w1, w3, w2)


def _ple_kernel(x_ref, y0_ref, y1_ref, wt_ref, g_ref, p_ref, wg_ref, wp_ref, out_ref, x2_sc, hn_sc, *, tn):
    j = pl.program_id(1)

    @pl.when(j == 0)
    def _():
        tm = x_ref.shape[0]
        rc = tm // COMBINE_CHUNKS
        for c0 in range(0, tm, rc):
            rows = slice(c0, c0 + rc)
            wt = wt_ref[rows, :]
            x2 = x_ref[rows, :] + (wt[:, 0:1] * y0_ref[rows, :] + wt[:, 1:2] * y1_ref[rows, :])
            hn_sc[rows, :] = _rms(x2, g_ref[...]).astype(BF16)
            for c in range(x2_sc.shape[0]):
                x2_sc[c, rows, :] = x2[:, c * tn:(c + 1) * tn]

    gate = jax.nn.sigmoid(jnp.dot(hn_sc[...], wg_ref[...], preferred_element_type=F32))
    proj = jnp.dot(p_ref[...].astype(BF16), wp_ref[...], preferred_element_type=F32)
    out_ref[...] = x2_sc[j] + gate * proj


def _ple(x, y_slots, wts, g, p, wg, wp, *, layer):
    T, D = x.shape
    dp = p.shape[2]
    tm = _row_tile(T, 528, 16 * COMBINE_CHUNKS)
    tn = 512
    nt = T // tm
    return pl.pallas_call(
        functools.partial(_ple_kernel, tn=tn),
        grid=(nt, D // tn),
        in_specs=[
            pl.BlockSpec((tm, D), lambda i, j: (i, 0)),
            pl.BlockSpec((tm, D), lambda i, j: (i, 0)),
            pl.BlockSpec((tm, D), lambda i, j: (i + nt, 0)),
            pl.BlockSpec((tm, TOP_K), lambda i, j: (i, 0)),
            _layer_vec(layer, D),
            pl.BlockSpec((None, tm, dp), lambda i, j: (layer, i, 0)),
            pl.BlockSpec((None, D, tn), lambda i, j: (layer, 0, j)),
            pl.BlockSpec((None, dp, tn), lambda i, j: (layer, 0, j)),
        ],
        out_specs=pl.BlockSpec((tm, tn), lambda i, j: (i, j)),
        out_shape=jax.ShapeDtypeStruct((T, D), F32),
        scratch_shapes=[pltpu.VMEM((D // tn, tm, tn), F32), pltpu.VMEM((tm, D), BF16)],
        compiler_params=_params(("parallel", "arbitrary")),
        name="ple",
    )(x, y_slots, y_slots, wts, g, p, wg, wp)


def _cast_kernel(w_ref, o_ref):
    o_ref[...] = w_ref[...].astype(o_ref.dtype)


def _to_bf16(w):
    depth, rows, cols = w.shape
    rb = _row_tile(rows, 512)
    spec = pl.BlockSpec((None, rb, cols), lambda l, i: (l, i, 0))
    return pl.pallas_call(
        _cast_kernel,
        grid=(depth, rows // rb),
        in_specs=[spec],
        out_specs=spec,
        out_shape=jax.ShapeDtypeStruct(w.shape, BF16),
        compiler_params=_params(("parallel", "parallel")),
        name="to_bf16",
    )(w)


def kernel(x_prompt, x_sample, cache_k, cache_v, state_rnn_h, state_conv, p_prompt, p_sample, norm_mix, w_in, q_norm, k_norm, lambda_q1, lambda_k1, lambda_q2, lambda_k2, attn_out_norm, conv_w, conv_b, w_rg_a, b_rg_a, w_rg_x, b_rg_x, rg_lambda, rnn_out_norm, w_out, norm_ffn, w_router_group, b_router_group, w_router_expert, b_router_expert, w_exp_gate, w_exp_up, w_exp_down, norm_ple, w_ple_gate, w_ple_proj):
    bp, seq, D = x_prompt.shape
    nb, dseq, _ = x_sample.shape
    depth = w_in.shape[0]
    d_rnn = state_rnn_h.shape[-1]
    d_ple = p_prompt.shape[-1]
    assert bp == 1 and seq % CHUNK == 0 and dseq % 16 == 0 and seq % dseq == 0
    n_p = bp * seq
    n_s = nb * dseq
    T = n_p + n_s

    x = jnp.concatenate([x_prompt.reshape(n_p, D), x_sample.reshape(n_s, D)], axis=0)
    p_all = jnp.concatenate([p_prompt.reshape(depth, n_p, d_ple), p_sample.reshape(depth, n_s, d_ple)], axis=1)
    vec = lambda a: a.reshape(depth, 1, -1)
    conv_pad = jnp.pad(state_conv, ((0, 0), (0, 0), (SUBLANES - (CONV_W - 1), 0), (0, 0)))
    h0_s = state_rnn_h.reshape(depth, nb, 1, d_rnn)
    zero_h = jnp.zeros((1, bp, 1, d_rnn), F32)
    zero_conv = jnp.zeros((1, bp, SUBLANES, d_rnn), F32)
    pad_lanes = ROUTER_LANES - N_GROUPS - N_EXPERTS
    w_router = jnp.concatenate([w_router_group, w_router_expert, jnp.zeros((depth, D, pad_lanes), F32)], axis=-1)
    b_router = jnp.concatenate([b_router_group, b_router_expert, jnp.zeros((depth, pad_lanes), F32)], axis=-1)
    tb_p = _row_tile(seq, 256, SUBLANES)
    w_in, w_out, w_ple_gate, w_ple_proj = (_to_bf16(w) for w in (w_in, w_out, w_ple_gate, w_ple_proj))

    norm_mix_v, qg_v, kg_v = vec(norm_mix), vec(q_norm), vec(k_norm)
    lam_params = (vec(lambda_q1), vec(lambda_k1), vec(lambda_q2), vec(lambda_k2), vec(attn_out_norm))
    rnn_w = (conv_w, vec(conv_b), w_rg_a, vec(b_rg_a), w_rg_x, vec(b_rg_x), vec(rg_lambda), vec(rnn_out_norm))
    norm_ffn_v, b_router_v, norm_ple_v = vec(norm_ffn), vec(b_router), vec(norm_ple)

    ks, vs = [], []
    hp, cp, hs_, cs_ = [], [], [], []
    for l in range(depth):
        lam0 = _lambda_init(l)
        q, k, v, xr, gate, kb, vb = _in_proj(x, norm_mix_v, w_in, qg_v, kg_v, layer=l)
        ks.append(k)
        vs.append(v)
        o = jnp.zeros((T, D_ATTN), BF16)
        o = _attn_prompt(q, kb, vb, lam_params, qg_v, kg_v, o, layer=l, n_prompt=n_p, lam0=lam0)
        o = _attn_sample(q, kb, vb, cache_k, cache_v, lam_params, o, layer=l, n_prompt=n_p, rows=dseq, lam0=lam0)

        y = jnp.zeros((T, d_rnn), BF16)
        y, h_p, c_p = _rnn(xr, gate, zero_h, zero_conv, rnn_w, y, layer=l, state_layer=0,
                           row0=0, n_seq=bp, seq_len=seq, tb=tb_p)
        y, h_s, c_s = _rnn(xr, gate, h0_s, conv_pad, rnn_w, y, layer=l, state_layer=l,
                           row0=n_p, n_seq=nb, seq_len=dseq, tb=dseq)

        x = _out_proj(o, y, w_out, x, layer=l)

        hn, logits = _router(x, norm_ffn_v, w_router, b_router_v, layer=l)
        blk_exp, row_src, row_dst, n_used, groups, wts = _routing_tables(logits, T)
        y_slots = _moe(hn, blk_exp, row_src, row_dst, n_used, groups, w_exp_gate, w_exp_up, w_exp_down, layer=l)

        x = _ple(x, y_slots, wts, norm_ple_v, p_all, w_ple_gate, w_ple_proj, layer=l)

        hp.append(h_p[:, 0])
        cp.append(c_p[:, SUBLANES - (CONV_W - 1):])
        hs_.append(h_s[:, 0])
        cs_.append(c_s[:, SUBLANES - (CONV_W - 1):])

    k_p, k_s, v_p, v_s = _kv_layout(ks, vs, n_prompt=n_p, n_sample=n_s)
    return (x[:n_p].reshape(bp, seq, D), x[n_p:].reshape(nb, dseq, D),
            k_p.reshape(depth, bp, seq, N_HEADS, QK_DIM), v_p.reshape(depth, bp, seq, N_HEADS, V_DIM),
            jnp.stack(hp), jnp.stack(cp),
            k_s.reshape(depth, nb, dseq, N_HEADS, QK_DIM), v_s.reshape(depth, nb, dseq, N_HEADS, V_DIM),
            jnp.stack(hs_), jnp.stack(cs_))
```

```python
import functools
import math

import jax
import jax.numpy as jnp
from jax import lax
from jax.experimental import pallas as pl
from jax.experimental.pallas import tpu as pltpu

F32 = jnp.float32
BF16 = jnp.bfloat16

CHUNK = 64
N_HEADS = 8
HEAD_DIM = 64
QK_DIM = 2 * HEAD_DIM
V_DIM = 2 * HEAD_DIM
D_ATTN = N_HEADS * V_DIM
N_RNN_BLOCKS = 8
CONV_W = 4
RG_C = 8.0
N_GROUPS = 4
EXPERTS_PER_GROUP = 8
N_EXPERTS = N_GROUPS * EXPERTS_PER_GROUP
TOP_K = 2
EPS = 1e-6
NEG_INF = -1e30

LANES = 128
SUBLANES = 8
VMEM_LIMIT_BYTES = 56 * 1024 * 1024
MOE_ROWS = 256
ROUTER_LANES = 128
ROW_DMA_UNROLL = 8
COMBINE_CHUNKS = 3


def _lambda_init(layer):
    return 0.8 - 0.6 * math.exp(-0.3 * layer)


def _row_tile(total, target, multiple=16):
    best = None
    for t in range(multiple, min(total, target) + 1, multiple):
        if total % t == 0:
            best = t
    assert best is not None, (total, target)
    return best


def _params(sem):
    return pltpu.CompilerParams(dimension_semantics=sem, vmem_limit_bytes=VMEM_LIMIT_BYTES)


def _layer_vec(l, n):
    return pl.BlockSpec((None, 1, n), lambda *_: (l, 0, 0))


def _rms(x, gain):
    ms = jnp.mean(x * x, axis=-1, keepdims=True)
    return x * lax.rsqrt(ms + EPS) * gain


def _half_norm(a, gain, lo):
    sq = a * a
    s_lo = jnp.sum(jnp.where(lo, sq, 0.0), axis=-1, keepdims=True)
    s_hi = jnp.sum(jnp.where(lo, 0.0, sq), axis=-1, keepdims=True)
    ms = jnp.where(lo, s_lo, s_hi) * (1.0 / HEAD_DIM)
    return a * lax.rsqrt(ms + EPS) * gain


def _in_proj_kernel(x_ref, g_ref, w_ref, qg_ref, kg_ref,
                    q_ref, k_ref, v_ref, xr_ref, gate_ref, kb_ref, vb_ref, hn_ref, *, steps_per_section, tn):
    j = pl.program_id(1)

    @pl.when(j == 0)
    def _():
        hn_ref[...] = _rms(x_ref[...], g_ref[...]).astype(BF16)

    acc = jnp.dot(hn_ref[...], w_ref[...], preferred_element_type=F32)
    sec = j // steps_per_section
    lo = lax.broadcasted_iota(jnp.int32, (1, LANES), 1) < HEAD_DIM

    @pl.when(sec == 0)
    def _():
        for h in range(tn // LANES):
            sl = slice(h * LANES, (h + 1) * LANES)
            q_ref[:, sl] = (_half_norm(acc[:, sl], qg_ref[...], lo) * (HEAD_DIM ** -0.5)).astype(BF16)

    @pl.when(sec == 1)
    def _():
        for h in range(tn // LANES):
            sl = slice(h * LANES, (h + 1) * LANES)
            kn = _half_norm(acc[:, sl], kg_ref[...], lo)
            k_ref[:, sl] = kn
            kb_ref[:, sl] = kn.astype(BF16)

    @pl.when(sec == 2)
    def _():
        v_ref[...] = acc
        vb_ref[...] = acc.astype(BF16)

    @pl.when(sec == 3)
    def _():
        xr_ref[...] = acc

    @pl.when(sec == 4)
    def _():
        gate_ref[...] = acc


def _in_proj(x, g, w_in, qg, kg, *, layer):
    T, D = x.shape
    tm = _row_tile(T, 1056)
    tn = 512
    sec_w = D_ATTN
    sps = sec_w // tn
    n_steps = w_in.shape[2] // tn
    col = lambda sec: (lambda j: jnp.clip(j - sec * sps, 0, sps - 1))

    def out_spec(sec):
        return pl.BlockSpec((tm, tn), lambda i, j: (i, col(sec)(j)))

    flat = jax.ShapeDtypeStruct((T, sec_w), F32)
    flat_b = jax.ShapeDtypeStruct((T, sec_w), BF16)
    return pl.pallas_call(
        functools.partial(_in_proj_kernel, steps_per_section=sps, tn=tn),
        grid=(T // tm, n_steps),
        in_specs=[
            pl.BlockSpec((tm, D), lambda i, j: (i, 0)),
            _layer_vec(layer, D),
            pl.BlockSpec((None, D, tn), lambda i, j: (layer, 0, j)),
            _layer_vec(layer, LANES),
            _layer_vec(layer, LANES),
        ],
        out_specs=[out_spec(s) for s in (0, 1, 2, 3, 4, 1, 2)],
        out_shape=[flat_b, flat, flat, flat, flat, flat_b, flat_b],
        scratch_shapes=[pltpu.VMEM((tm, D), BF16)],
        compiler_params=_params(("parallel", "arbitrary")),
        name="in_proj",
    )(x, g, w_in, qg, kg)


def _kv_layout_kernel(*refs, depth, prompt_tiles):
    srcs = refs[:2 * depth]
    kp_ref, ks_ref, vp_ref, vs_ref = refs[2 * depth:]
    l = pl.program_id(0)
    i = pl.program_id(1)

    def spread(src, dst):
        for h in range(N_HEADS):
            dst[:, h, :] = src[:, h * LANES:(h + 1) * LANES]

    for d in range(depth):
        @pl.when(jnp.logical_and(l == d, i < prompt_tiles))
        def _():
            spread(srcs[d], kp_ref)
            spread(srcs[depth + d], vp_ref)

        @pl.when(jnp.logical_and(l == d, i >= prompt_tiles))
        def _():
            spread(srcs[d], ks_ref)
            spread(srcs[depth + d], vs_ref)


def _kv_layout(ks, vs, *, n_prompt, n_sample):
    depth = len(ks)
    T, width = ks[0].shape
    tr = math.gcd(n_prompt, n_sample)
    nt, np_t, ns_t = T // tr, n_prompt // tr, n_sample // tr

    def src_spec(d):
        return pl.BlockSpec((tr, width), lambda l, i: (jnp.where(l < d, 0, jnp.where(l > d, nt - 1, i)), 0))

    blk = (None, tr, N_HEADS, LANES)
    prompt_spec = pl.BlockSpec(blk, lambda l, i: (l, jnp.minimum(i, np_t - 1), 0, 0))
    sample_spec = pl.BlockSpec(blk, lambda l, i: (l, jnp.clip(i - np_t, 0, ns_t - 1), 0, 0))
    shape = lambda n: jax.ShapeDtypeStruct((depth, n, N_HEADS, LANES), F32)
    return pl.pallas_call(
        functools.partial(_kv_layout_kernel, depth=depth, prompt_tiles=np_t),
        grid=(depth, nt),
        in_specs=[src_spec(d) for d in range(depth)] * 2,
        out_specs=[prompt_spec, sample_spec, prompt_spec, sample_spec],
        out_shape=[shape(n_prompt), shape(n_sample), shape(n_prompt), shape(n_sample)],
        compiler_params=_params(("arbitrary", "arbitrary")),
        name="kv_layout",
    )(*ks, *vs)


def _lambda_value(lq1_ref, lk1_ref, lq2_ref, lk2_ref, lam0):
    s1 = jnp.sum(lq1_ref[...] * lk1_ref[...], axis=-1, keepdims=True)
    s2 = jnp.sum(lq2_ref[...] * lk2_ref[...], axis=-1, keepdims=True)
    return jnp.exp(s1) - jnp.exp(s2) + lam0


def _split_q(q):
    lo = lax.broadcasted_iota(jnp.int32, q.shape, 1) < HEAD_DIM
    zero = jnp.zeros_like(q)
    return jnp.concatenate([jnp.where(lo, q, zero), jnp.where(lo, zero, q)], axis=0)


def _scores(q2, kb):
    return lax.dot_general(q2, kb, (((1,), (1,)), ((), ())), preferred_element_type=F32)


def _finish_heads(acc, l, lam, gout, rows, lam0):
    o = acc[:rows] / l[:rows] - lam * (acc[rows:] / l[rows:])
    return _rms(o, gout) * (1.0 - lam0)


def _attn_prompt_kernel(q_ref, k_ref, v_ref, lq1_ref, lk1_ref, lq2_ref, lk2_ref, gout_ref, o_init_ref,
                        o_ref, q2_sc, m_sc, acc_sc, *, bq, lam0):
    del o_init_ref
    i = pl.program_id(1)
    bk = bq
    q2_sc[...] = _split_q(q_ref[...])
    m_sc[...] = jnp.full(m_sc.shape, NEG_INF, F32)
    acc_sc[...] = jnp.zeros(acc_sc.shape, F32)
    ones = jnp.ones((bk, LANES), BF16)

    def update(off, masked):
        kb = k_ref[pl.ds(off, bk), :]
        vb = v_ref[pl.ds(off, bk), :]
        s = _scores(q2_sc[...], kb)
        if masked:
            q_chunk = (lax.broadcasted_iota(jnp.int32, s.shape, 0) % bq) // CHUNK
            k_chunk = lax.broadcasted_iota(jnp.int32, s.shape, 1) // CHUNK
            s = jnp.where(k_chunk <= q_chunk, s, NEG_INF)
        m_old = m_sc[...]
        m_new = jnp.maximum(m_old, jnp.max(s, axis=-1, keepdims=True))
        pr = jnp.exp(s - jnp.concatenate([m_new] * (bk // LANES), axis=1))
        alpha = jnp.exp(m_old - m_new)
        pv = jnp.dot(pr.astype(BF16), jnp.concatenate([vb, ones], axis=1), preferred_element_type=F32)
        acc_sc[...] = jnp.concatenate([alpha, alpha], axis=1) * acc_sc[...] + pv
        m_sc[...] = m_new

    def body(j, carry):
        update(pl.multiple_of(j * bk, bk), False)
        return carry

    lax.fori_loop(0, i, body, 0)
    update(pl.multiple_of(i * bk, bk), True)
    lam = _lambda_value(lq1_ref, lk1_ref, lq2_ref, lk2_ref, lam0)
    acc = acc_sc[...]
    o_ref[...] = _finish_heads(acc[:, :LANES], acc[:, LANES:], lam, gout_ref[...], bq, lam0).astype(o_ref.dtype)


def _score_bound(qg_ref, kg_ref):
    gq = jnp.max(jnp.abs(qg_ref[...]), axis=-1, keepdims=True)
    gk = jnp.max(jnp.abs(kg_ref[...]), axis=-1, keepdims=True)
    return (HEAD_DIM ** 0.5) * gq * gk


def _attn_prompt_shift_kernel(q_ref, k_ref, v_ref, lq1_ref, lk1_ref, lq2_ref, lk2_ref, gout_ref, qg_ref, kg_ref,
                              o_init_ref, o_ref, q2_sc, acc_sc, *, bq, lam0):
    del o_init_ref
    i = pl.program_id(1)
    bk = bq
    hb = bq // 2
    for a in range(2):
        q2_sc[a * bq:(a + 1) * bq, :] = _split_q(q_ref[a * hb:(a + 1) * hb, :])
    acc_sc[...] = jnp.zeros(acc_sc.shape, F32)
    shift = _score_bound(qg_ref, kg_ref)

    def accumulate(rows, off, n_keys, mask_from_half):
        kb = k_ref[pl.ds(off, n_keys), :]
        vb = v_ref[pl.ds(off, n_keys), :]
        s = _scores(q2_sc[rows, :], kb) - shift
        if mask_from_half is not None:
            q_chunk = (lax.broadcasted_iota(jnp.int32, s.shape, 0) % hb + mask_from_half * hb) // CHUNK
            k_chunk = lax.broadcasted_iota(jnp.int32, s.shape, 1) // CHUNK
            s = jnp.where(k_chunk <= q_chunk, s, NEG_INF)
        pr = jnp.exp(s).astype(BF16)
        ones = jnp.ones((n_keys, LANES), BF16)
        acc_sc[rows, :] += jnp.dot(pr, jnp.concatenate([vb, ones], axis=1), preferred_element_type=F32)

    everything = slice(0, 2 * bq)

    def body(jp, carry):
        accumulate(everything, pl.multiple_of(2 * jp * bk, bk), bk, None)
        accumulate(everything, pl.multiple_of((2 * jp + 1) * bk, bk), bk, None)
        return carry

    lax.fori_loop(0, i // 2, body, 0)

    @pl.when(i % 2 == 1)
    def _():
        accumulate(everything, pl.multiple_of((i - 1) * bk, bk), bk, None)

    diag = pl.multiple_of(i * bk, bk)
    accumulate(slice(0, bq), diag, hb, 0)
    accumulate(slice(bq, 2 * bq), diag, bk, 1)
    lam = _lambda_value(lq1_ref, lk1_ref, lq2_ref, lk2_ref, lam0)
    for a in range(2):
        acc = acc_sc[a * bq:(a + 1) * bq, :]
        o_ref[a * hb:(a + 1) * hb, :] = _finish_heads(
            acc[:, :LANES], acc[:, LANES:], lam, gout_ref[...], hb, lam0).astype(o_ref.dtype)


MAX_FIXED_SHIFT = 30.0


def _attn_prompt(q, kb, vb, lam_params, qg, kg, o_init, *, layer, n_prompt, lam0):
    bq = _row_tile(n_prompt, 1024, 2 * LANES)
    common_in = [
        pl.BlockSpec((bq, LANES), lambda h, i: (i, h)),
        pl.BlockSpec((n_prompt, LANES), lambda h, i: (0, h)),
        pl.BlockSpec((n_prompt, LANES), lambda h, i: (0, h)),
        _layer_vec(layer, HEAD_DIM), _layer_vec(layer, HEAD_DIM), _layer_vec(layer, HEAD_DIM),
        _layer_vec(layer, HEAD_DIM), _layer_vec(layer, V_DIM),
    ]
    common = dict(
        grid=(N_HEADS, n_prompt // bq),
        out_specs=pl.BlockSpec((bq, LANES), lambda h, i: (i, h)),
        out_shape=jax.ShapeDtypeStruct(o_init.shape, o_init.dtype),
        compiler_params=_params(("parallel", "arbitrary")),
    )
    q2_scratch = pltpu.VMEM((2 * bq, LANES), BF16)
    acc_scratch = pltpu.VMEM((2 * bq, 2 * LANES), F32)

    def streaming(o):
        return pl.pallas_call(
            functools.partial(_attn_prompt_kernel, bq=bq, lam0=lam0),
            in_specs=common_in + [pl.BlockSpec(memory_space=pl.ANY)],
            scratch_shapes=[q2_scratch, pltpu.VMEM((2 * bq, LANES), F32), acc_scratch],
            input_output_aliases={8: 0},
            name="attn_prompt",
            **common,
        )(q, kb, vb, *lam_params, o)

    def fixed_shift(o):
        return pl.pallas_call(
            functools.partial(_attn_prompt_shift_kernel, bq=bq, lam0=lam0),
            in_specs=common_in + [_layer_vec(layer, LANES), _layer_vec(layer, LANES),
                                  pl.BlockSpec(memory_space=pl.ANY)],
            scratch_shapes=[q2_scratch, acc_scratch],
            input_output_aliases={10: 0},
            name="attn_prompt_shift",
            **common,
        )(q, kb, vb, *lam_params, qg, kg, o)

    bound = (HEAD_DIM ** 0.5) * jnp.max(jnp.abs(qg[layer])) * jnp.max(jnp.abs(kg[layer]))
    return lax.cond(bound <= MAX_FIXED_SHIFT, fixed_shift, streaming, o_init)


def _attn_sample_kernel(q_ref, k_ref, v_ref, ck_ref, cv_ref, lq1_ref, lk1_ref, lq2_ref, lk2_ref, gout_ref,
                        o_init_ref, o_ref, *, rows, lam0):
    del o_init_ref
    lam = _lambda_value(lq1_ref, lk1_ref, lq2_ref, lk2_ref, lam0)
    for h in range(N_HEADS):
        sl = slice(h * LANES, (h + 1) * LANES)
        q2 = _split_q(q_ref[:, sl])
        s_c = _scores(q2, ck_ref[:, h, :].astype(BF16))
        s_n = _scores(q2, k_ref[:, sl])
        m = jnp.maximum(jnp.max(s_c, axis=-1, keepdims=True), jnp.max(s_n, axis=-1, keepdims=True))
        p_c = jnp.exp(s_c - m)
        p_n = jnp.exp(s_n - m)
        l = jnp.sum(p_c, axis=-1, keepdims=True) + jnp.sum(p_n, axis=-1, keepdims=True)
        acc = (jnp.dot(p_c.astype(BF16), cv_ref[:, h, :].astype(BF16), preferred_element_type=F32)
               + jnp.dot(p_n.astype(BF16), v_ref[:, sl], preferred_element_type=F32))
        o_ref[:, sl] = _finish_heads(acc, l, lam, gout_ref[...], rows, lam0).astype(o_ref.dtype)


def _attn_sample(q, kb, vb, cache_k, cache_v, lam_params, o_init, *, layer, n_prompt, rows, lam0):
    n_streams, past = cache_k.shape[1], cache_k.shape[2]
    base = n_prompt // rows
    row_spec = pl.BlockSpec((rows, D_ATTN), lambda b: (base + b, 0))
    cache_spec = pl.BlockSpec((None, None, past, N_HEADS, LANES), lambda b: (layer, b, 0, 0, 0))
    return pl.pallas_call(
        functools.partial(_attn_sample_kernel, rows=rows, lam0=lam0),
        grid=(n_streams,),
        in_specs=[row_spec, row_spec, row_spec, cache_spec, cache_spec,
                  _layer_vec(layer, HEAD_DIM), _layer_vec(layer, HEAD_DIM), _layer_vec(layer, HEAD_DIM),
                  _layer_vec(layer, HEAD_DIM), _layer_vec(layer, V_DIM),
                  pl.BlockSpec(memory_space=pl.ANY)],
        out_specs=row_spec,
        out_shape=jax.ShapeDtypeStruct(o_init.shape, o_init.dtype),
        input_output_aliases={10: 0},
        compiler_params=_params(("parallel",)),
        name="attn_sample",
    )(q, kb, vb, cache_k, cache_v, *lam_params, o_init)


def _gelu_tanh(x):
    return 0.5 * x * (1.0 + jnp.tanh(math.sqrt(2.0 / math.pi) * (x + 0.044715 * (x * x * x))))


def _rnn_kernel(xr_ref, gate_ref, h0_ref, cbuf_ref, cw_ref, cb_ref, wa_ref, ba_ref, wx_ref, bx_ref,
                lam_ref, gn_ref, y_init_ref, y_ref, hlast_ref, ctail_ref, h_sc, win_sc, *, tb):
    del y_init_ref
    t = pl.program_id(1)

    @pl.when(t == 0)
    def _():
        h_sc[...] = h0_ref[...]
        win_sc[0:SUBLANES, :] = cbuf_ref[...]

    x = xr_ref[...]
    cw = cw_ref[...]
    win_sc[SUBLANES:SUBLANES + tb, :] = x
    xc = cb_ref[...] + x * cw[CONV_W - 1:CONV_W]
    for back in range(1, CONV_W):
        xc = xc + win_sc[SUBLANES - back:SUBLANES - back + tb, :] * cw[CONV_W - 1 - back:CONV_W - back]
    win_sc[0:SUBLANES, :] = x[tb - SUBLANES:]
    ctail_ref[...] = x[tb - SUBLANES:]

    xb = xc.astype(BF16)
    r_parts, i_parts = [], []
    for n in range(N_RNN_BLOCKS):
        sl = slice(n * LANES, (n + 1) * LANES)
        r_parts.append(jnp.dot(xb[:, sl], wa_ref[n].astype(BF16), preferred_element_type=F32))
        i_parts.append(jnp.dot(xb[:, sl], wx_ref[n].astype(BF16), preferred_element_type=F32))
    r = jax.nn.sigmoid(jnp.concatenate(r_parts, axis=-1) + ba_ref[...])
    ig = jax.nn.sigmoid(jnp.concatenate(i_parts, axis=-1) + bx_ref[...])
    neg_lam = -lam_ref[...]
    softplus = jnp.maximum(neg_lam, 0.0) + jnp.log1p(jnp.exp(-jnp.abs(neg_lam)))
    log_a = (-RG_C * r) * softplus
    a = jnp.exp(log_a)
    u = jnp.sqrt(1.0 - a * a) * (ig * xc)

    row = lax.broadcasted_iota(jnp.int32, (tb, 1), 0)
    d = 1
    while d < tb:
        keep = row >= d
        u = jnp.where(keep, a * pltpu.roll(u, d, axis=0) + u, u)
        a = jnp.where(keep, a * pltpu.roll(a, d, axis=0), a)
        d *= 2
    hs = u + a * h_sc[...]
    h_sc[...] = hs[tb - 1:tb]
    hlast_ref[...] = hs[tb - 1:tb]

    y_ref[...] = _rms(hs * _gelu_tanh(gate_ref[...]), gn_ref[...]).astype(y_ref.dtype)


def _rnn(xr, gate, h0, cbuf, weights, y_init, *, layer, state_layer, row0, n_seq, seq_len, tb):
    C = xr.shape[1]
    nb = seq_len // tb
    base = row0 // tb
    row_spec = pl.BlockSpec((tb, C), lambda b, t: (base + b * nb + t, 0))
    blk = lambda: pl.BlockSpec((None, N_RNN_BLOCKS, LANES, LANES), lambda b, t: (layer, 0, 0, 0))
    return pl.pallas_call(
        functools.partial(_rnn_kernel, tb=tb),
        grid=(n_seq, nb),
        in_specs=[row_spec, row_spec,
                  pl.BlockSpec((None, None, 1, C), lambda b, t: (state_layer, b, 0, 0)),
                  pl.BlockSpec((None, None, SUBLANES, C), lambda b, t: (state_layer, b, 0, 0)),
                  pl.BlockSpec((None, CONV_W, C), lambda b, t: (layer, 0, 0)), _layer_vec(layer, C),
                  blk(), _layer_vec(layer, C), blk(), _layer_vec(layer, C), _layer_vec(layer, C),
                  _layer_vec(layer, C),
                  pl.BlockSpec(memory_space=pl.ANY)],
        out_specs=[row_spec,
                   pl.BlockSpec((None, 1, C), lambda b, t: (b, 0, 0)),
                   pl.BlockSpec((None, SUBLANES, C), lambda b, t: (b, 0, 0))],
        out_shape=[jax.ShapeDtypeStruct(y_init.shape, y_init.dtype),
                   jax.ShapeDtypeStruct((n_seq, 1, C), F32),
                   jax.ShapeDtypeStruct((n_seq, SUBLANES, C), F32)],
        scratch_shapes=[pltpu.VMEM((1, C), F32), pltpu.VMEM((SUBLANES + tb, C), F32)],
        input_output_aliases={12: 0},
        compiler_params=_params(("arbitrary", "arbitrary")),
        name="rnn",
    )(xr, gate, h0, cbuf, *weights, y_init)


def _out_proj_kernel(o_ref, y_ref, wa_ref, wb_ref, x_ref, out_ref):
    acc = jnp.dot(o_ref[...], wa_ref[...], preferred_element_type=F32)
    acc = acc + jnp.dot(y_ref[...], wb_ref[...], preferred_element_type=F32)
    out_ref[...] = x_ref[...] + acc


def _out_proj(o, y, w_out, x, *, layer):
    T, D = x.shape
    half = o.shape[1]
    tm = _row_tile(T, 1408)
    tn = 512
    return pl.pallas_call(
        _out_proj_kernel,
        grid=(T // tm, D // tn),
        in_specs=[
            pl.BlockSpec((tm, half), lambda i, j: (i, 0)),
            pl.BlockSpec((tm, half), lambda i, j: (i, 0)),
            pl.BlockSpec((None, half, tn), lambda i, j: (layer, 0, j)),
            pl.BlockSpec((None, half, tn), lambda i, j: (layer, 1, j)),
            pl.BlockSpec((tm, tn), lambda i, j: (i, j)),
        ],
        out_specs=pl.BlockSpec((tm, tn), lambda i, j: (i, j)),
        out_shape=jax.ShapeDtypeStruct((T, D), F32),
        compiler_params=_params(("parallel", "arbitrary")),
        name="out_proj",
    )(o, y, w_out, w_out, x)


def _router_kernel(x_ref, g_ref, w_ref, b_ref, hn_ref, logit_ref):
    hn = _rms(x_ref[...], g_ref[...])
    hn_ref[...] = hn
    logit_ref[...] = jnp.dot(hn.astype(BF16), w_ref[...].astype(BF16), preferred_element_type=F32) + b_ref[...]


def _router(x, g, w, b, *, layer):
    T, D = x.shape
    tm = _row_tile(T, 768)
    return pl.pallas_call(
        _router_kernel,
        grid=(T // tm,),
        in_specs=[pl.BlockSpec((tm, D), lambda i: (i, 0)),
                  _layer_vec(layer, D),
                  pl.BlockSpec((None, D, ROUTER_LANES), lambda i: (layer, 0, 0)),
                  _layer_vec(layer, ROUTER_LANES)],
        out_specs=[pl.BlockSpec((tm, D), lambda i: (i, 0)),
                   pl.BlockSpec((tm, ROUTER_LANES), lambda i: (i, 0))],
        out_shape=[jax.ShapeDtypeStruct((T, D), F32), jax.ShapeDtypeStruct((T, ROUTER_LANES), F32)],
        compiler_params=_params(("parallel",)),
        name="router",
    )(x, g, w, b)


def _top1(x):
    val = jnp.max(x, axis=-1, keepdims=True)
    n = x.shape[-1]
    idx = jnp.min(jnp.where(x == val, lax.broadcasted_iota(jnp.int32, x.shape, x.ndim - 1), n), axis=-1,
                  keepdims=True)
    return val, idx


def _routing_tables(logits, n_tokens):
    T = n_tokens
    gp = jax.nn.softmax(logits[:, :N_GROUPS], axis=-1)
    g_val, g_idx = _top1(gp)
    el = logits[:, N_GROUPS:N_GROUPS + N_EXPERTS].reshape(T, N_GROUPS, EXPERTS_PER_GROUP)
    group = lax.broadcasted_iota(jnp.int32, el.shape, 1)
    el_g = jnp.sum(jnp.where(group == g_idx[:, :, None], el, 0.0), axis=1)
    v0, i0 = _top1(el_g)
    lane = lax.broadcasted_iota(jnp.int32, el_g.shape, 1)
    v1, i1 = _top1(jnp.where(lane == i0, -jnp.inf, el_g))
    e_val = jnp.concatenate([v0, v1], axis=-1)
    e_idx = jnp.concatenate([i0, i1], axis=-1)
    wts = g_val * jax.nn.softmax(e_val, axis=-1)
    eid = (g_idx * EXPERTS_PER_GROUP + e_idx).astype(jnp.int32)

    A = T * TOP_K
    B = MOE_ROWS
    n_blocks = -(-(A + N_EXPERTS * (B - 1)) // B)
    n_rows = n_blocks * B
    eid_f = eid.reshape(A)
    onehot = (eid_f[:, None] == jnp.arange(N_EXPERTS, dtype=jnp.int32)[None, :]).astype(jnp.int32)
    csum = jnp.cumsum(onehot, axis=0)
    rank = jnp.sum(onehot * csum, axis=1) - 1
    counts = csum[-1]
    padded = (counts + B - 1) // B * B
    pad_end = jnp.cumsum(padded)
    pad_start = pad_end - padded
    dest = jnp.sum(onehot * pad_start[None, :], axis=1) + rank
    row_a = jnp.full((n_rows,), -1, jnp.int32).at[dest].set(jnp.arange(A, dtype=jnp.int32))
    tok = jnp.maximum(row_a, 0) // TOP_K
    row_src = tok
    rows = jnp.arange(n_rows, dtype=jnp.int32)
    dump = TOP_K * T + ((rows // B) % 2) * B + rows % B
    row_dst = jnp.where(row_a < 0, dump, (row_a % TOP_K) * T + tok)
    blk_exp = jnp.minimum(
        jnp.sum((pad_end[None, :] <= (jnp.arange(n_blocks, dtype=jnp.int32) * B)[:, None]).astype(jnp.int32), axis=1),
        N_EXPERTS - 1).astype(jnp.int32)
    n_used = (pad_end[-1] // B).astype(jnp.int32).reshape(1)
    n_real = jnp.sum((row_a >= 0).reshape(n_blocks, B).astype(jnp.int32), axis=1)
    groups = (n_real + ROW_DMA_UNROLL - 1) // ROW_DMA_UNROLL
    blocks = jnp.arange(n_blocks, dtype=jnp.int32)
    prev_exp = jnp.concatenate([jnp.full((1,), -1, jnp.int32), blk_exp[:-1]])
    first = jnp.logical_and(blocks < n_used[0], blk_exp != prev_exp)
    run = jnp.cumsum(first.astype(jnp.int32)) - 1
    starts = lax.cummin(jnp.where(first, blocks, n_blocks), axis=0, reverse=True)
    nxt = jnp.concatenate([starts[1:], jnp.full((1,), n_blocks, jnp.int32)])
    next_exp = jnp.sum(jnp.where(blocks[None, :] == nxt[:, None], blk_exp[None, :], 0), axis=1)
    plan = (first.astype(jnp.int32) | ((run % 2) << 1) | ((nxt < n_blocks).astype(jnp.int32) << 2)
            | (next_exp << 3)).astype(jnp.int32)
    return blk_exp, row_src, row_dst, n_used, groups, plan, wts


def _moe_kernel(blk_ref, src_ref, dst_ref, nused_ref, grp_ref, plan_ref, hn_ref, w1_hbm, w3_hbm, w2_hbm,
                y_ref, xg_sc, yo_sc, w1_sc, w3_sc, w2_sc, gsem, ssem, wsem, *, n_blocks, layer):
    B = MOE_ROWS
    n_real = y_ref.shape[0] - 2 * B
    i = pl.program_id(0)
    n_used = nused_ref[0]
    slot = i % 2
    plan = plan_ref[i]
    wslot = (plan >> 1) & 1

    def weight_copies(e, s):
        return [pltpu.make_async_copy(w_hbm.at[layer, e], w_sc.at[s], wsem.at[s])
                for w_hbm, w_sc in ((w1_hbm, w1_sc), (w3_hbm, w3_sc), (w2_hbm, w2_sc))]

    @pl.when(i == 0)
    def _():
        for c in weight_copies(blk_ref[0], 0):
            c.start()

    @pl.when((plan & 1) == 1)
    def _():
        for c in weight_copies(blk_ref[i], wslot):
            c.wait()

        @pl.when(((plan >> 2) & 1) == 1)
        def _():
            for c in weight_copies(plan >> 3, 1 - wslot):
                c.start()

    def for_each_group(blk, issue):
        n_groups = grp_ref[blk]
        for g in range(B // ROW_DMA_UNROLL):
            @pl.when(g < n_groups)
            def _():
                for u in range(ROW_DMA_UNROLL):
                    issue(g * ROW_DMA_UNROLL + u)

    def start_gather(blk, s):
        def issue(r):
            src = src_ref[blk * B + r]
            pltpu.make_async_copy(hn_ref.at[pl.ds(src, 1)], xg_sc.at[s, pl.ds(r, 1)],
                                  gsem.at[s]).start(priority=r % 2)
        for_each_group(blk, issue)

    def wait_gather(blk, s):
        def body(g, c):
            pltpu.make_async_copy(hn_ref.at[pl.ds(0, ROW_DMA_UNROLL)], xg_sc.at[s, pl.ds(0, ROW_DMA_UNROLL)],
                                  gsem.at[s]).wait()
            return c
        lax.fori_loop(0, grp_ref[blk], body, 0)

    def start_scatter(blk, s):
        def issue(r):
            dst = dst_ref[blk * B + r]
            pltpu.make_async_copy(yo_sc.at[s, pl.ds(r, 1)], y_ref.at[pl.ds(dst, 1)],
                                  ssem.at[s]).start(priority=r % 2)
        for_each_group(blk, issue)

    def wait_scatter(blk, s):
        def body(g, c):
            pltpu.make_async_copy(yo_sc.at[s, pl.ds(0, ROW_DMA_UNROLL)], y_ref.at[pl.ds(0, ROW_DMA_UNROLL)],
                                  ssem.at[s]).wait()
            return c
        lax.fori_loop(0, grp_ref[blk], body, 0)

    @pl.when(i == 0)
    def _():
        xg_sc[...] = jnp.zeros(xg_sc.shape, F32)
        start_gather(0, 0)
        yo_sc[0] = jnp.zeros(yo_sc.shape[1:], F32)
        for half in range(2):
            fill = pltpu.make_async_copy(yo_sc.at[0], y_ref.at[pl.ds(n_real + half * B, B)], ssem.at[0])
            fill.start()
            fill.wait()

    @pl.when(i + 1 < n_used)
    def _():
        start_gather(i + 1, 1 - slot)

    @pl.when(i < n_used)
    def _():
        wait_gather(i, slot)

        @pl.when(i >= 2)
        def _():
            wait_scatter(i - 2, slot)

        xb = xg_sc[slot].astype(BF16)
        h1 = jnp.dot(xb, w1_sc[wslot].astype(BF16), preferred_element_type=F32)
        h3 = jnp.dot(xb, w3_sc[wslot].astype(BF16), preferred_element_type=F32)
        h = (h1 * jax.nn.sigmoid(h1) * h3).astype(BF16)
        yo_sc[slot] = jnp.dot(h, w2_sc[wslot].astype(BF16), preferred_element_type=F32)
        start_scatter(i, slot)

    @pl.when(i == n_blocks - 1)
    def _():
        wait_scatter(n_used - 1, (n_used - 1) % 2)

        @pl.when(n_used >= 2)
        def _():
            wait_scatter(n_used - 2, n_used % 2)


def _moe(hn, blk_exp, row_src, row_dst, n_used, groups, plan, w1, w3, w2, *, layer):
    T, D = hn.shape
    B = MOE_ROWS
    n_blocks = blk_exp.shape[0]
    de = w1.shape[3]
    grid_spec = pltpu.PrefetchScalarGridSpec(
        num_scalar_prefetch=6,
        grid=(n_blocks,),
        in_specs=[pl.BlockSpec(memory_space=pl.ANY)] * 4,
        out_specs=pl.BlockSpec(memory_space=pl.ANY),
        scratch_shapes=[
            pltpu.VMEM((2, B, D), F32),
            pltpu.VMEM((2, B, D), F32),
            pltpu.VMEM((2, D, de), F32),
            pltpu.VMEM((2, D, de), F32),
            pltpu.VMEM((2, de, D), F32),
            pltpu.SemaphoreType.DMA((2,)),
            pltpu.SemaphoreType.DMA((2,)),
            pltpu.SemaphoreType.DMA((2,)),
        ],
    )
    return pl.pallas_call(
        functools.partial(_moe_kernel, n_blocks=n_blocks, layer=layer),
        grid_spec=grid_spec,
        out_shape=jax.ShapeDtypeStruct((TOP_K * T + 2 * B, D), F32),
        compiler_params=_params(("arbitrary",)),
        name="moe",
    )(blk_exp, row_src, row_dst, n_used, groups, plan, hn, w1, w3, w2)


def _ple_kernel(x_ref, y0_ref, y1_ref, wt_ref, g_ref, p_ref, wg_ref, wp_ref, out_ref, x2_sc, hn_sc, *, tn):
    j = pl.program_id(1)

    @pl.when(j == 0)
    def _():
        tm = x_ref.shape[0]
        rc = tm // COMBINE_CHUNKS
        for c0 in range(0, tm, rc):
            rows = slice(c0, c0 + rc)
            wt = wt_ref[rows, :]
            x2 = x_ref[rows, :] + (wt[:, 0:1] * y0_ref[rows, :] + wt[:, 1:2] * y1_ref[rows, :])
            hn_sc[rows, :] = _rms(x2, g_ref[...]).astype(BF16)
            for c in range(x2_sc.shape[0]):
                x2_sc[c, rows, :] = x2[:, c * tn:(c + 1) * tn]

    gate = jax.nn.sigmoid(jnp.dot(hn_sc[...], wg_ref[...], preferred_element_type=F32))
    proj = jnp.dot(p_ref[...].astype(BF16), wp_ref[...], preferred_element_type=F32)
    out_ref[...] = x2_sc[j] + gate * proj


def _ple(x, y_slots, wts, g, p, wg, wp, *, layer):
    T, D = x.shape
    dp = p.shape[2]
    tm = _row_tile(T, 528, 16 * COMBINE_CHUNKS)
    tn = 512
    nt = T // tm
    return pl.pallas_call(
        functools.partial(_ple_kernel, tn=tn),
        grid=(nt, D // tn),
        in_specs=[
            pl.BlockSpec((tm, D), lambda i, j: (i, 0)),
            pl.BlockSpec((tm, D), lambda i, j: (i, 0)),
            pl.BlockSpec((tm, D), lambda i, j: (i + nt, 0)),
            pl.BlockSpec((tm, TOP_K), lambda i, j: (i, 0)),
            _layer_vec(layer, D),
            pl.BlockSpec((None, tm, dp), lambda i, j: (layer, i, 0)),
            pl.BlockSpec((None, D, tn), lambda i, j: (layer, 0, j)),
            pl.BlockSpec((None, dp, tn), lambda i, j: (layer, 0, j)),
        ],
        out_specs=pl.BlockSpec((tm, tn), lambda i, j: (i, j)),
        out_shape=jax.ShapeDtypeStruct((T, D), F32),
        scratch_shapes=[pltpu.VMEM((D // tn, tm, tn), F32), pltpu.VMEM((tm, D), BF16)],
        compiler_params=_params(("parallel", "arbitrary")),
        name="ple",
    )(x, y_slots, y_slots, wts, g, p, wg, wp)


def _cast_kernel(w_ref, o_ref):
    o_ref[...] = w_ref[...].astype(o_ref.dtype)


def _to_bf16(w):
    depth, rows, cols = w.shape
    rb = _row_tile(rows, 512)
    spec = pl.BlockSpec((None, rb, cols), lambda l, i: (l, i, 0))
    return pl.pallas_call(
        _cast_kernel,
        grid=(depth, rows // rb),
        in_specs=[spec],
        out_specs=spec,
        out_shape=jax.ShapeDtypeStruct(w.shape, BF16),
        compiler_params=_params(("parallel", "parallel")),
        name="to_bf16",
    )(w)


def kernel(x_prompt, x_sample, cache_k, cache_v, state_rnn_h, state_conv, p_prompt, p_sample, norm_mix, w_in, q_norm, k_norm, lambda_q1, lambda_k1, lambda_q2, lambda_k2, attn_out_norm, conv_w, conv_b, w_rg_a, b_rg_a, w_rg_x, b_rg_x, rg_lambda, rnn_out_norm, w_out, norm_ffn, w_router_group, b_router_group, w_router_expert, b_router_expert, w_exp_gate, w_exp_up, w_exp_down, norm_ple, w_ple_gate, w_ple_proj):
    bp, seq, D = x_prompt.shape
    nb, dseq, _ = x_sample.shape
    depth = w_in.shape[0]
    d_rnn = state_rnn_h.shape[-1]
    d_ple = p_prompt.shape[-1]
    assert bp == 1 and seq % CHUNK == 0 and dseq % 16 == 0 and seq % dseq == 0
    n_p = bp * seq
    n_s = nb * dseq
    T = n_p + n_s

    x = jnp.concatenate([x_prompt.reshape(n_p, D), x_sample.reshape(n_s, D)], axis=0)
    p_all = jnp.concatenate([p_prompt.reshape(depth, n_p, d_ple), p_sample.reshape(depth, n_s, d_ple)], axis=1)
    vec = lambda a: a.reshape(depth, 1, -1)
    conv_pad = jnp.pad(state_conv, ((0, 0), (0, 0), (SUBLANES - (CONV_W - 1), 0), (0, 0)))
    h0_s = state_rnn_h.reshape(depth, nb, 1, d_rnn)
    zero_h = jnp.zeros((1, bp, 1, d_rnn), F32)
    zero_conv = jnp.zeros((1, bp, SUBLANES, d_rnn), F32)
    pad_lanes = ROUTER_LANES - N_GROUPS - N_EXPERTS
    w_router = jnp.concatenate([w_router_group, w_router_expert, jnp.zeros((depth, D, pad_lanes), F32)], axis=-1)
    b_router = jnp.concatenate([b_router_group, b_router_expert, jnp.zeros((depth, pad_lanes), F32)], axis=-1)
    tb_p = _row_tile(seq, 256, SUBLANES)
    w_in, w_out, w_ple_gate, w_ple_proj = (_to_bf16(w) for w in (w_in, w_out, w_ple_gate, w_ple_proj))

    norm_mix_v, qg_v, kg_v = vec(norm_mix), vec(q_norm), vec(k_norm)
    lam_params = (vec(lambda_q1), vec(lambda_k1), vec(lambda_q2), vec(lambda_k2), vec(attn_out_norm))
    rnn_w = (conv_w, vec(conv_b), w_rg_a, vec(b_rg_a), w_rg_x, vec(b_rg_x), vec(rg_lambda), vec(rnn_out_norm))
    norm_ffn_v, b_router_v, norm_ple_v = vec(norm_ffn), vec(b_router), vec(norm_ple)

    ks, vs = [], []
    hp, cp, hs_, cs_ = [], [], [], []
    for l in range(depth):
        lam0 = _lambda_init(l)
        q, k, v, xr, gate, kb, vb = _in_proj(x, norm_mix_v, w_in, qg_v, kg_v, layer=l)
        ks.append(k)
        vs.append(v)
        o = jnp.zeros((T, D_ATTN), BF16)
        o = _attn_prompt(q, kb, vb, lam_params, qg_v, kg_v, o, layer=l, n_prompt=n_p, lam0=lam0)
        o = _attn_sample(q, kb, vb, cache_k, cache_v, lam_params, o, layer=l, n_prompt=n_p, rows=dseq, lam0=lam0)

        y = jnp.zeros((T, d_rnn), BF16)
        y, h_p, c_p = _rnn(xr, gate, zero_h, zero_conv, rnn_w, y, layer=l, state_layer=0,
                           row0=0, n_seq=bp, seq_len=seq, tb=tb_p)
        y, h_s, c_s = _rnn(xr, gate, h0_s, conv_pad, rnn_w, y, layer=l, state_layer=l,
                           row0=n_p, n_seq=nb, seq_len=dseq, tb=dseq)

        x = _out_proj(o, y, w_out, x, layer=l)

        hn, logits = _router(x, norm_ffn_v, w_router, b_router_v, layer=l)
        blk_exp, row_src, row_dst, n_used, groups, plan, wts = _routing_tables(logits, T)
        y_slots = _moe(hn, blk_exp, row_src, row_dst, n_used, groups, plan, w_exp_gate, w_exp_up, w_exp_down,
                       layer=l)

        x = _ple(x, y_slots, wts, norm_ple_v, p_all, w_ple_gate, w_ple_proj, layer=l)

        hp.append(h_p[:, 0])
        cp.append(c_p[:, SUBLANES - (CONV_W - 1):])
        hs_.append(h_s[:, 0])
        cs_.append(c_s[:, SUBLANES - (CONV_W - 1):])

    k_p, k_s, v_p, v_s = _kv_layout(ks, vs, n_prompt=n_p, n_sample=n_s)
    return (x[:n_p].reshape(bp, seq, D), x[n_p:].reshape(nb, dseq, D),
            k_p.reshape(depth, bp, seq, N_HEADS, QK_DIM), v_p.reshape(depth, bp, seq, N_HEADS, V_DIM),
            jnp.stack(hp), jnp.stack(cp),
            k_s.reshape(depth, nb, dseq, N_HEADS, QK_DIM), v_s.reshape(depth, nb, dseq, N_HEADS, V_DIM),
            jnp.stack(hs_), jnp.stack(cs_))
```
